```python
import jax, jax.numpy as jnp
from jax import lax
import numpy as np

D_MODEL = 1024
BATCH = 8
SEQ = 2048
DEPTH = 4
DEC_BATCH = 128
DEC_SEQ = 1
PAST_LEN = 8192
PAGE_SIZE = 128

N_A_LAYERS = DEPTH // 2
N_B_LAYERS = DEPTH - N_A_LAYERS
CHUNK = 128
SGU_WIDTH = 2 * D_MODEL
SGU_GROUPS = 8
SGU_GROUP_DIM = SGU_WIDTH // SGU_GROUPS
N_HEADS = 8
QK_NOPE = 128
QK_ROPE = 64
V_HEAD = 128
Q_LORA = D_MODEL // 2
KV_LORA = D_MODEL // 4
ROPE_THETA = 10000.0
Q_BLOCK = 128
ATTN_SCALE = (QK_NOPE + QK_ROPE) ** -0.5
D_FF = 11 * D_MODEL // 4
CONV_W = 3
EPS = 1e-6

kernel_name = "yoco_sgu_mla_convffn_step"


def rms_norm(x, g):
    xf = x.astype(jnp.float32)
    y = xf * lax.rsqrt(jnp.mean(xf * xf, axis=-1, keepdims=True) + EPS)
    return (y * g.astype(jnp.float32)).astype(x.dtype)


def rope(x, pos):
    half = QK_ROPE // 2
    inv = 1.0 / (ROPE_THETA ** (jnp.arange(half, dtype=jnp.float32) / half))
    ang = pos.astype(jnp.float32)[:, None] * inv[None, :]
    shp = (ang.shape[0],) + (1,) * (x.ndim - 3) + (half,)
    cos, sin = jnp.cos(ang).reshape(shp), jnp.sin(ang).reshape(shp)
    xf = x.astype(jnp.float32)
    x1, x2 = xf[..., :half], xf[..., half:]
    return jnp.concatenate([x1 * cos - x2 * sin, x2 * cos + x1 * sin], axis=-1).astype(x.dtype)


def sgu_mixer(h, w_in, sgu_g, w_s, b_s, w_out):
    n, t, _ = h.shape
    z = jax.nn.gelu(h @ w_in, approximate=False)
    u, v = jnp.split(z, 2, axis=-1)
    v = rms_norm(v, sgu_g)
    cl = CHUNK if t >= CHUNK else t
    pad = (-t) % cl
    n_chunks = (t + pad) // cl
    vp = jnp.pad(v, ((0, 0), (0, pad), (0, 0))).reshape(n, n_chunks, cl, SGU_GROUPS, SGU_GROUP_DIM)
    causal = jnp.tril(jnp.ones((cl, cl), dtype=bool))
    wm = jnp.where(causal[None], w_s[:, :cl, :cl], 0)
    s = jnp.einsum('gts,ncsgd->nctgd', wm, vp) + b_s[:, :cl].T[:, :, None]
    s = s.reshape(n, n_chunks * cl, SGU_WIDTH)[:, :t]
    return (u * s) @ w_out, v


def conv_ffn(h, conv_prev, w_up, conv_w, conv_b, w_down):
    t = h.shape[1]
    a = h @ w_up
    ap = jnp.concatenate([conv_prev.astype(a.dtype), a], axis=1)
    c = conv_b + ap[:, 0:t] * conv_w[0]
    for k in range(1, CONV_W):
        c = c + ap[:, k:k + t] * conv_w[k]
    g, u = jnp.split(c, 2, axis=-1)
    return (jax.nn.gelu(g, approximate=True) * u) @ w_down, ap[:, t:]


def kv_side(h, pos, kv_in_g, w_dkv, kv_g):
    kv = rms_norm(h, kv_in_g) @ w_dkv
    return rms_norm(kv[..., :KV_LORA], kv_g), rope(kv[..., KV_LORA:], pos)


def mla_attend(q_lat, q_pe, q_pos, ckv, kr, k_pos):
    n, t = q_lat.shape[:2]
    qb = Q_BLOCK if t % Q_BLOCK == 0 else t
    nb = t // qb

    def block(args):
        ql, qp, pq = args
        s = (jnp.einsum('nthc,nsc->nhts', ql, ckv, preferred_element_type=jnp.float32)
             + jnp.einsum('nthr,nsr->nhts', qp, kr, preferred_element_type=jnp.float32)) * ATTN_SCALE
        s = jnp.where(k_pos[None, None, None, :] <= pq[None, None, :, None], s, -jnp.inf)
        p = jax.nn.softmax(s, axis=-1).astype(ckv.dtype)
        return jnp.einsum('nhts,nsc->nthc', p, ckv)

    qlb = q_lat.reshape(n, nb, qb, N_HEADS, KV_LORA).transpose(1, 0, 2, 3, 4)
    qpb = q_pe.reshape(n, nb, qb, N_HEADS, QK_ROPE).transpose(1, 0, 2, 3, 4)
    out = lax.map(block, (qlb, qpb, q_pos.reshape(nb, qb)))
    return out.transpose(1, 0, 2, 3, 4).reshape(n, t, N_HEADS, KV_LORA)


def mla_mixer(h, pos, ckv, kr, k_pos, w_dq, q_g, w_uq, w_uk, w_uv, w_o):
    n, t, _ = h.shape
    cq = rms_norm(h @ w_dq, q_g)
    q = (cq @ w_uq).reshape(n, t, N_HEADS, QK_NOPE + QK_ROPE)
    q_pe = rope(q[..., QK_NOPE:], pos)
    q_lat = jnp.einsum('nthd,chd->nthc', q[..., :QK_NOPE], w_uk)
    o_lat = mla_attend(q_lat, q_pe, pos, ckv, kr, k_pos)
    o = jnp.einsum('nthc,chd->nthd', o_lat, w_uv).reshape(n, t, N_HEADS * V_HEAD)
    return o @ w_o


def run_group(x, pos, past_ckv, past_kr, conv_prev, p):
    v_rows, conv_new = [], []
    ckv_new = kr_new = ckv_all = kr_all = k_pos = None
    for layer in range(DEPTH):
        if layer == N_A_LAYERS:
            ckv_new, kr_new = kv_side(x, pos, p['kv_in_g'], p['w_dkv'], p['kv_g'])
            if past_ckv is None:
                ckv_all, kr_all, k_pos = ckv_new, kr_new, pos
            else:
                past_len = past_ckv.shape[1]
                ckv_all = jnp.concatenate([past_ckv.astype(ckv_new.dtype), ckv_new], axis=1)
                kr_all = jnp.concatenate([past_kr.astype(kr_new.dtype), kr_new], axis=1)
                k_pos = jnp.concatenate([jnp.arange(past_len, dtype=jnp.int32), pos])
        h = rms_norm(x, p['pre_mix_g'][layer])
        if layer < N_A_LAYERS:
            m, v = sgu_mixer(h, p['w_in_a'][layer], p['sgu_g'][layer], p['w_s'][layer],
                             p['b_s'][layer], p['w_out_a'][layer])
            v_rows.append(v)
        else:
            j = layer - N_A_LAYERS
            m = mla_mixer(h, pos, ckv_all, kr_all, k_pos, p['w_dq'][j], p['q_g'][j], p['w_uq'][j],
                          p['w_uk'], p['w_uv'], p['w_o'][j])
        x = x + rms_norm(m, p['post_mix_g'][layer])
        h = rms_norm(x, p['pre_ffn_g'][layer])
        f, cs = conv_ffn(h, conv_prev[layer], p['w_up'][layer], p['conv_w'][layer],
                         p['conv_b'][layer], p['w_down'][layer])
        conv_new.append(cs)
        x = x + rms_norm(f, p['post_ffn_g'][layer])
    return x, ckv_new, kr_new, jnp.stack(conv_new), jnp.stack(v_rows)


def setup_inputs(seed: int = 0) -> dict:
    key = jax.random.key(seed)
    ks = iter(jax.random.split(key, 40))
    f32 = jnp.float32

    def nrm(shape, scale):
        return jax.random.normal(next(ks), shape, f32) * scale

    def gain(shape):
        return 1.0 + 0.1 * jax.random.normal(next(ks), shape, f32)

    n_pages = PAST_LEN // PAGE_SIZE
    n_used = DEC_BATCH * n_pages
    n_pool = n_used + max(1, n_used // 4)
    page_table = jax.random.permutation(next(ks), n_pool)[:n_used].reshape(DEC_BATCH, n_pages).astype(jnp.int32)
    return {
        'x_prompt': nrm((BATCH, SEQ, D_MODEL), 1.0),
        'x_sample': nrm((DEC_BATCH, DEC_SEQ, D_MODEL), 1.0),
        'cache_ckv': nrm((n_pool, PAGE_SIZE, KV_LORA), 1.0),
        'cache_kr': nrm((n_pool, PAGE_SIZE, QK_ROPE), 1.0),
        'state_conv': nrm((DEPTH, DEC_BATCH, CONV_W - 1, 2 * D_FF), 1.0),
        'page_table': page_table,
        'pre_mix_g': gain((DEPTH, D_MODEL)),
        'post_mix_g': gain((DEPTH, D_MODEL)),
        'pre_ffn_g': gain((DEPTH, D_MODEL)),
        'post_ffn_g': gain((DEPTH, D_MODEL)),
        'w_in_a': nrm((N_A_LAYERS, D_MODEL, 2 * SGU_WIDTH), D_MODEL ** -0.5),
        'sgu_g': gain((N_A_LAYERS, SGU_WIDTH)),
        'w_s': nrm((N_A_LAYERS, SGU_GROUPS, CHUNK, CHUNK), CHUNK ** -0.5),
        'b_s': gain((N_A_LAYERS, SGU_GROUPS, CHUNK)),
        'w_out_a': nrm((N_A_LAYERS, SGU_WIDTH, D_MODEL), SGU_WIDTH ** -0.5),
        'kv_in_g': gain((D_MODEL,)),
        'w_dkv': nrm((D_MODEL, KV_LORA + QK_ROPE), D_MODEL ** -0.5),
        'kv_g': gain((KV_LORA,)),
        'w_uk': nrm((KV_LORA, N_HEADS, QK_NOPE), KV_LORA ** -0.5),
        'w_uv': nrm((KV_LORA, N_HEADS, V_HEAD), KV_LORA ** -0.5),
        'w_dq': nrm((N_B_LAYERS, D_MODEL, Q_LORA), D_MODEL ** -0.5),
        'q_g': gain((N_B_LAYERS, Q_LORA)),
        'w_uq': nrm((N_B_LAYERS, Q_LORA, N_HEADS * (QK_NOPE + QK_ROPE)), Q_LORA ** -0.5),
        'w_o': nrm((N_B_LAYERS, N_HEADS * V_HEAD, D_MODEL), (N_HEADS * V_HEAD) ** -0.5),
        'w_up': nrm((DEPTH, D_MODEL, 2 * D_FF), D_MODEL ** -0.5),
        'conv_w': nrm((DEPTH, CONV_W, 2 * D_FF), CONV_W ** -0.5),
        'conv_b': nrm((DEPTH, 2 * D_FF), 0.01),
        'w_down': nrm((DEPTH, D_FF, D_MODEL), D_FF ** -0.5),
    }


def reference(x_prompt, x_sample, cache_ckv, cache_kr, state_conv, page_table,
              pre_mix_g, post_mix_g, pre_ffn_g, post_ffn_g,
              w_in_a, sgu_g, w_s, b_s, w_out_a,
              kv_in_g, w_dkv, kv_g, w_uk, w_uv,
              w_dq, q_g, w_uq, w_o,
              w_up, conv_w, conv_b, w_down):
    p = dict(pre_mix_g=pre_mix_g, post_mix_g=post_mix_g, pre_ffn_g=pre_ffn_g, post_ffn_g=post_ffn_g,
             w_in_a=w_in_a, sgu_g=sgu_g, w_s=w_s, b_s=b_s, w_out_a=w_out_a,
             kv_in_g=kv_in_g, w_dkv=w_dkv, kv_g=kv_g, w_uk=w_uk, w_uv=w_uv,
             w_dq=w_dq, q_g=q_g, w_uq=w_uq, w_o=w_o,
             w_up=w_up, conv_w=conv_w, conv_b=conv_b, w_down=w_down)

    bp, tp, _ = x_prompt.shape
    pos_p = jnp.arange(tp, dtype=jnp.int32)
    conv0 = jnp.zeros((DEPTH, bp, CONV_W - 1, 2 * D_FF), x_prompt.dtype)
    y_prompt, new_ckv_prompt, new_kr_prompt, new_conv_prompt, _ = run_group(
        x_prompt, pos_p, None, None, conv0, p)

    nb, n_pages = page_table.shape
    past_len = n_pages * PAGE_SIZE
    past_ckv = cache_ckv[page_table].reshape(nb, past_len, KV_LORA)
    past_kr = cache_kr[page_table].reshape(nb, past_len, QK_ROPE)
    pos_s = past_len + jnp.arange(x_sample.shape[1], dtype=jnp.int32)
    y_sample, new_ckv_sample, new_kr_sample, new_conv_sample, new_chunkv_sample = run_group(
        x_sample, pos_s, past_ckv, past_kr, state_conv, p)

    return (y_prompt, y_sample, new_ckv_prompt, new_kr_prompt, new_conv_prompt,
            new_ckv_sample, new_kr_sample, new_conv_sample, new_chunkv_sample)
```

```python
import functools

import jax
import jax.numpy as jnp
from jax import lax
from jax.experimental import pallas as pl
from jax.experimental.pallas import tpu as pltpu

F32 = jnp.float32
BF16 = jnp.bfloat16

D_MODEL = 1024
N_A_LAYERS = 2
CHUNK = 128
SGU_WIDTH = 2 * D_MODEL
SGU_GROUPS = 8
SGU_GROUP_DIM = SGU_WIDTH // SGU_GROUPS
N_HEADS = 8
QK_NOPE = 128
QK_ROPE = 64
V_HEAD = 128
Q_LORA = D_MODEL // 2
KV_LORA = D_MODEL // 4
ROPE_THETA = 10000.0
ATTN_SCALE = (QK_NOPE + QK_ROPE) ** -0.5
D_FF = 11 * D_MODEL // 4
CONV_W = 3
EPS = 1e-6
PAGE_SIZE = 128

LANES = 128
BF16_ROWS = 16
VMEM_LIMIT = 56 * 1024 * 1024

KCAT = KV_LORA + LANES
FF_TILE = 256
N_FF = D_FF // FF_TILE
TM_FFN = 512
TM_SGU = 512
TM_KV = 512
TQ = 256
DEC_CHUNK_PAGES = 8


def _rms(x, g):
    return x * lax.rsqrt(jnp.mean(x * x, axis=-1, keepdims=True) + EPS) * g


def _gelu_erf(x):
    return 0.5 * x * (1.0 + lax.erf(x * (0.5 ** 0.5)))


def _dot(a, b):
    return jnp.dot(a, b, preferred_element_type=F32)


def _dot_nt(a, b):
    return lax.dot_general(a, b, (((1,), (1,)), ((), ())), preferred_element_type=F32)


def _params(n_grid_axes):
    return pltpu.CompilerParams(
        dimension_semantics=("arbitrary",) * n_grid_axes, vmem_limit_bytes=VMEM_LIMIT)


def _resident(shape):
    nd = len(shape)
    return pl.BlockSpec(shape, lambda *_: (0,) * nd, pipeline_mode=pl.Buffered(1))


def _ffn_prompt_kernel(x_ref, halo_ref, pre_g_ref, post_g_ref, wup_ref, cw_ref, cb_ref, wdn_ref,
                       y_ref, tail_ref, acc_ref):
    tm = x_ref.shape[1]
    i = pl.program_id(1)
    x = x_ref[0]
    g = pre_g_ref[...]
    h = _rms(x, g)
    hh = jnp.where(i > 0, _rms(halo_ref[0], g), 0.0)
    hall = jnp.concatenate([hh, h], axis=0).astype(BF16)

    def conv(col0):
        sl = slice(col0, col0 + FF_TILE)
        a = _dot(hall, wup_ref[:, sl])
        w = cw_ref[:, sl]
        c = cb_ref[:, sl] + pltpu.roll(a, 2, 0)[BF16_ROWS:] * w[0:1]
        c = c + pltpu.roll(a, 1, 0)[BF16_ROWS:] * w[1:2]
        c = c + a[BF16_ROWS:] * w[2:3]
        tail_ref[0, 0, :, sl] = a[tm + 8:]
        return c

    for f in range(N_FF):
        cg = conv(f * FF_TILE)
        cu = conv(D_FF + f * FF_TILE)
        act = (jax.nn.gelu(cg, approximate=True) * cu).astype(BF16)
        part = _dot(act, wdn_ref[f * FF_TILE:(f + 1) * FF_TILE, :])
        if f == 0:
            acc_ref[...] = part
        else:
            acc_ref[...] += part
    y_ref[0] = x + _rms(acc_ref[...], post_g_ref[...])


def _ffn_prompt(x, pre_g, post_g, wup, cw, cb, wdn):
    b, t, d = x.shape
    tm = TM_FFN
    nt = t // tm
    halo_blocks = tm // BF16_ROWS
    y, tail = pl.pallas_call(
        _ffn_prompt_kernel,
        grid=(b, nt),
        in_specs=[
            pl.BlockSpec((1, tm, d), lambda bi, i: (bi, i, 0)),
            pl.BlockSpec((1, BF16_ROWS, d), lambda bi, i: (bi, jnp.maximum(i * halo_blocks - 1, 0), 0)),
            _resident((1, d)), _resident((1, d)),
            _resident(wup.shape), _resident(cw.shape), _resident(cb.shape), _resident(wdn.shape),
        ],
        out_specs=[
            pl.BlockSpec((1, tm, d), lambda bi, i: (bi, i, 0)),
            pl.BlockSpec((1, 1, 8, 2 * D_FF), lambda bi, i: (bi, i, 0, 0)),
        ],
        out_shape=[
            jax.ShapeDtypeStruct((b, t, d), F32),
            jax.ShapeDtypeStruct((b, nt, 8, 2 * D_FF), F32),
        ],
        scratch_shapes=[pltpu.VMEM((tm, d), F32)],
        compiler_params=_params(2),
        name="ffn_prompt",
    )(x, x, pre_g, post_g, wup, cw, cb, wdn)
    return y, tail[:, nt - 1, 8 - (CONV_W - 1):, :]


def _ffn_sample_kernel(x_ref, p0g_ref, p0u_ref, p1g_ref, p1u_ref, pre_g_ref, post_g_ref,
                       wg_ref, wu_ref, cwg_ref, cwu_ref, cbg_ref, cbu_ref, wdn_ref,
                       y_ref, ag_ref, au_ref, h_ref, acc_ref):
    f = pl.program_id(0)

    @pl.when(f == 0)
    def _():
        h_ref[...] = _rms(x_ref[...], pre_g_ref[...]).astype(BF16)

    h = h_ref[...]

    def conv(w_ref, p0_ref, p1_ref, cw_ref, cb_ref, a_ref):
        a = _dot(h, w_ref[...])
        a_ref[...] = a
        w = cw_ref[...]
        c = cb_ref[...] + p0_ref[...] * w[0:1]
        c = c + p1_ref[...] * w[1:2]
        return c + a * w[2:3]

    cg = conv(wg_ref, p0g_ref, p1g_ref, cwg_ref, cbg_ref, ag_ref)
    cu = conv(wu_ref, p0u_ref, p1u_ref, cwu_ref, cbu_ref, au_ref)
    act = (jax.nn.gelu(cg, approximate=True) * cu).astype(BF16)
    part = _dot(act, wdn_ref[...])

    @pl.when(f == 0)
    def _():
        acc_ref[...] = part

    @pl.when(f > 0)
    def _():
        acc_ref[...] += part

    @pl.when(f == pl.num_programs(0) - 1)
    def _():
        y_ref[...] = x_ref[...] + _rms(acc_ref[...], post_g_ref[...])


def _ffn_sample(x, state, pre_g, post_g, wup, cw, cb, wdn):
    n, d = x.shape
    tf = FF_TILE
    st = state.reshape(n, (CONV_W - 1) * 2 * D_FF)
    hist = 2 * D_FF // tf
    col = lambda off: pl.BlockSpec((n, tf), lambda f: (0, off + f))
    row3 = lambda off: pl.BlockSpec((CONV_W, tf), lambda f: (0, off + f))
    row1 = lambda off: pl.BlockSpec((1, tf), lambda f: (0, off + f))
    const = lambda shape: pl.BlockSpec(shape, lambda f: (0,) * len(shape))
    y, ag, au = pl.pallas_call(
        _ffn_sample_kernel,
        grid=(N_FF,),
        in_specs=[
            const((n, d)),
            col(0), col(N_FF), col(hist), col(hist + N_FF),
            const((1, d)), const((1, d)),
            pl.BlockSpec((d, tf), lambda f: (0, f)), pl.BlockSpec((d, tf), lambda f: (0, N_FF + f)),
            row3(0), row3(N_FF), row1(0), row1(N_FF),
            pl.BlockSpec((tf, d), lambda f: (f, 0)),
        ],
        out_specs=[const((n, d)), col(0), col(0)],
        out_shape=[
            jax.ShapeDtypeStruct((n, d), F32),
            jax.ShapeDtypeStruct((n, D_FF), F32),
            jax.ShapeDtypeStruct((n, D_FF), F32),
        ],
        scratch_shapes=[pltpu.VMEM((n, d), BF16), pltpu.VMEM((n, d), F32)],
        compiler_params=_params(1),
        name="ffn_sample",
    )(x, st, st, st, st, pre_g, post_g, wup, wup, cw, cw, cb, cb, wdn)
    new_state = jnp.stack([state[:, 1, :], jnp.concatenate([ag, au], axis=-1)], axis=1)
    return y, new_state


def _sgu_prompt_kernel(x_ref, pre_g_ref, post_g_ref, win_ref, sgug_ref, ws_ref, bs_ref, wout_ref,
                       y_ref, v_ref, acc_ref):
    tm = x_ref.shape[1]
    gd = SGU_GROUP_DIM
    x = x_ref[0]
    h = _rms(x, pre_g_ref[...]).astype(BF16)
    ss = jnp.zeros((tm, 1), F32)
    for j in range(SGU_GROUPS):
        zc = _gelu_erf(_dot(h, win_ref[:, SGU_WIDTH + j * gd:SGU_WIDTH + (j + 1) * gd]))
        v_ref[:, j * gd:(j + 1) * gd] = zc
        ss = ss + jnp.sum(zc * zc, axis=-1, keepdims=True)
    inv = lax.rsqrt(ss * (1.0 / SGU_WIDTH) + EPS)
    causal = (lax.broadcasted_iota(jnp.int32, (CHUNK, CHUNK), 0)
              >= lax.broadcasted_iota(jnp.int32, (CHUNK, CHUNK), 1))
    for g in range(SGU_GROUPS):
        sl = slice(g * gd, (g + 1) * gd)
        vn = (v_ref[:, sl] * inv * sgug_ref[:, sl]).astype(BF16)
        wm = jnp.where(causal, ws_ref[g], 0.0).astype(BF16)
        bias = bs_ref[g]
        s = jnp.concatenate(
            [_dot(wm, vn[c * CHUNK:(c + 1) * CHUNK]) + bias for c in range(tm // CHUNK)], axis=0)
        u = _gelu_erf(_dot(h, win_ref[:, sl]))
        part = _dot((u * s).astype(BF16), wout_ref[sl, :])
        if g == 0:
            acc_ref[...] = part
        else:
            acc_ref[...] += part
    y_ref[0] = x + _rms(acc_ref[...], post_g_ref[...])


def _sgu_prompt(x, pre_g, post_g, win, sgug, ws, bs, wout):
    b, t, d = x.shape
    tm = TM_SGU
    return pl.pallas_call(
        _sgu_prompt_kernel,
        grid=(b, t // tm),
        in_specs=[
            pl.BlockSpec((1, tm, d), lambda bi, i: (bi, i, 0)),
            _resident((1, d)), _resident((1, d)),
            _resident(win.shape), _resident(sgug.shape), _resident(ws.shape), _resident(bs.shape),
            _resident(wout.shape),
        ],
        out_specs=pl.BlockSpec((1, tm, d), lambda bi, i: (bi, i, 0)),
        out_shape=jax.ShapeDtypeStruct((b, t, d), F32),
        scratch_shapes=[pltpu.VMEM((tm, SGU_WIDTH), F32), pltpu.VMEM((tm, d), F32)],
        compiler_params=_params(2),
        name="sgu_prompt",
    )(x, pre_g, post_g, win, sgug, ws, bs, wout)


def _sgu_sample_kernel(x_ref, pre_g_ref, post_g_ref, win_ref, sgug_ref, wdiag_ref, bdiag_ref, wout_ref,
                       y_ref, v_ref):
    x = x_ref[...]
    h = _rms(x, pre_g_ref[...]).astype(BF16)
    u = _gelu_erf(_dot(h, win_ref[:, :SGU_WIDTH]))
    v = _gelu_erf(_dot(h, win_ref[:, SGU_WIDTH:]))
    vn = _rms(v, sgug_ref[...])
    v_ref[...] = vn
    s = vn * wdiag_ref[...] + bdiag_ref[...]
    y = _dot((u * s).astype(BF16), wout_ref[...])
    y_ref[...] = x + _rms(y, post_g_ref[...])


def _sgu_sample(x, pre_g, post_g, win, sgug, wdiag, bdiag, wout):
    n, d = x.shape
    return pl.pallas_call(
        _sgu_sample_kernel,
        out_shape=[jax.ShapeDtypeStruct((n, d), F32), jax.ShapeDtypeStruct((n, SGU_WIDTH), F32)],
        compiler_params=pltpu.CompilerParams(vmem_limit_bytes=VMEM_LIMIT),
        name="sgu_sample",
    )(x, pre_g, post_g, win, sgug, wdiag, bdiag, wout)


def _kv_side_kernel(x_ref, cs_ref, g_in_ref, wc_ref, wpe_ref, kvg_ref, ckv_ref, kr_ref, kcat_ref):
    h = _rms(x_ref[0], g_in_ref[...]).astype(BF16)
    ckv = _rms(_dot(h, wc_ref[...]), kvg_ref[...])
    t = _dot(h, wpe_ref[...]) * cs_ref[...]
    t = t + pltpu.roll(t, QK_ROPE, 1)
    lane = lax.broadcasted_iota(jnp.int32, t.shape, 1)
    krp = jnp.where(lane < QK_ROPE, t, 0.0)
    ckv_ref[0] = ckv
    kr_ref[0] = t[:, :QK_ROPE]
    kcat_ref[0] = jnp.concatenate([ckv, krp], axis=1).astype(BF16)


def _kv_side(x, cs_tab, g_in, wc, wpe, kvg, tm):
    b, t, d = x.shape
    return pl.pallas_call(
        _kv_side_kernel,
        grid=(b, t // tm),
        in_specs=[
            pl.BlockSpec((1, tm, d), lambda bi, i: (bi, i, 0)),
            pl.BlockSpec((tm, LANES), lambda bi, i: (i, 0)),
            _resident((1, d)), _resident(wc.shape), _resident(wpe.shape), _resident((1, KV_LORA)),
        ],
        out_specs=[
            pl.BlockSpec((1, tm, KV_LORA), lambda bi, i: (bi, i, 0)),
            pl.BlockSpec((1, tm, QK_ROPE), lambda bi, i: (bi, i, 0)),
            pl.BlockSpec((1, tm, KCAT), lambda bi, i: (bi, i, 0)),
        ],
        out_shape=[
            jax.ShapeDtypeStruct((b, t, KV_LORA), F32),
            jax.ShapeDtypeStruct((b, t, QK_ROPE), F32),
            jax.ShapeDtypeStruct((b, t, KCAT), BF16),
        ],
        compiler_params=_params(2),
        name="kv_side",
    )(x, cs_tab, g_in, wc, wpe, kvg)


def _q_rows(x, cos, sin, pre_g, wdq, qg, wuq_n, wuq_pa, wuq_pb, wuk_t, store):
    h = _rms(x, pre_g).astype(BF16)
    cq = _rms(_dot(h, wdq[...]), qg).astype(BF16)
    qn = _dot(cq, wuq_n[...])
    qa = _dot(cq, wuq_pa[...])
    qb = _dot(cq, wuq_pb[...])
    for hd in range(N_HEADS):
        sl = slice(hd * LANES, (hd + 1) * LANES)
        qpe = qa[:, sl] * cos + qb[:, sl] * sin
        ql = _dot(qn[:, sl].astype(BF16), wuk_t[hd])
        store(hd, (ql * ATTN_SCALE).astype(BF16), (qpe * ATTN_SCALE).astype(BF16))


def _mla_out(o_heads, x, post_g, wuv, wo, o_ref):
    for hd in range(N_HEADS):
        o_ref[:, hd * V_HEAD:(hd + 1) * V_HEAD] = _dot(o_heads(hd).astype(BF16), wuv[hd]).astype(BF16)
    m = _dot(o_ref[...], wo[...])
    return x + _rms(m, post_g)


def _mla_prompt_kernel(x_ref, k_ref, cos_ref, sin_ref, pre_g_ref, post_g_ref, wdq_ref, qg_ref,
                       wuqn_ref, wuqa_ref, wuqb_ref, wukt_ref, wuv_ref, wo_ref,
                       y_ref, q_ref, m_ref, l_ref, acc_ref, o_ref):
    tq = x_ref.shape[1]
    qi = pl.program_id(1)
    x = x_ref[0]

    def store(hd, ql, qpe):
        q_ref[hd * tq:(hd + 1) * tq, :KV_LORA] = ql
        q_ref[hd * tq:(hd + 1) * tq, KV_LORA:] = qpe

    _q_rows(x, cos_ref[...], sin_ref[...], pre_g_ref[...], wdq_ref, qg_ref[...],
            wuqn_ref, wuqa_ref, wuqb_ref, wukt_ref, store)

    m_ref[...] = jnp.full(m_ref.shape, -jnp.inf, F32)
    l_ref[...] = jnp.zeros(l_ref.shape, F32)
    acc_ref[...] = jnp.zeros(acc_ref.shape, F32)

    def step(kb, diagonal):
        k = k_ref[0, pl.ds(pl.multiple_of(kb * tq, tq), tq), :]
        s = _dot_nt(q_ref[...], k)
        if diagonal:
            row = lax.broadcasted_iota(jnp.int32, s.shape, 0) & (tq - 1)
            colm = lax.broadcasted_iota(jnp.int32, s.shape, 1)
            s = jnp.where(colm <= row, s, -jnp.inf)
        m_old = m_ref[...]
        m_new = jnp.maximum(m_old, jnp.max(s, axis=-1, keepdims=True))
        alpha = jnp.exp(m_old - m_new)
        p = jnp.exp(s - m_new)
        l_ref[...] = l_ref[...] * alpha + jnp.sum(p, axis=-1, keepdims=True)
        acc_ref[...] = acc_ref[...] * alpha + _dot(p.astype(BF16), k[:, :KV_LORA])
        m_ref[...] = m_new

    def body(kb, carry):
        step(kb, False)
        return carry

    lax.fori_loop(0, qi, body, 0)
    step(qi, True)

    inv_l = 1.0 / l_ref[...]
    y_ref[0] = _mla_out(lambda hd: acc_ref[hd * tq:(hd + 1) * tq, :] * inv_l[hd * tq:(hd + 1) * tq],
                        x, post_g_ref[...], wuv_ref, wo_ref, o_ref)


def _mla_prompt(x, kcat, cos, sin, pre_g, post_g, wdq, qg, wuq_n, wuq_pa, wuq_pb, wuk_t, wuv, wo):
    b, t, d = x.shape
    tq = TQ
    rows = N_HEADS * tq
    weights = (wdq, qg, wuq_n, wuq_pa, wuq_pb, wuk_t, wuv, wo)
    return pl.pallas_call(
        _mla_prompt_kernel,
        grid=(b, t // tq),
        in_specs=[
            pl.BlockSpec((1, tq, d), lambda bi, i: (bi, i, 0)),
            pl.BlockSpec((1, t, KCAT), lambda bi, i: (bi, 0, 0)),
            pl.BlockSpec((tq, LANES), lambda bi, i: (i, 0)),
            pl.BlockSpec((tq, LANES), lambda bi, i: (i, 0)),
            _resident((1, d)), _resident((1, d)),
        ] + [_resident(w.shape) for w in weights],
        out_specs=pl.BlockSpec((1, tq, d), lambda bi, i: (bi, i, 0)),
        out_shape=jax.ShapeDtypeStruct((b, t, d), F32),
        scratch_shapes=[
            pltpu.VMEM((rows, KCAT), BF16),
            pltpu.VMEM((rows, 1), F32), pltpu.VMEM((rows, 1), F32),
            pltpu.VMEM((rows, KV_LORA), F32),
            pltpu.VMEM((tq, N_HEADS * V_HEAD), BF16),
        ],
        compiler_params=_params(2),
        name="mla_prompt",
    )(x, kcat, cos, sin, pre_g, post_g, *weights)


def _q_sample_kernel(x_ref, cos_ref, sin_ref, pre_g_ref, wdq_ref, qg_ref, wuqn_ref, wuqa_ref, wuqb_ref,
                     wukt_ref, q_ref):
    def store(hd, ql, qpe):
        q_ref[hd, :, :KV_LORA] = ql
        q_ref[hd, :, KV_LORA:] = qpe

    _q_rows(x_ref[...], cos_ref[...], sin_ref[...], pre_g_ref[...], wdq_ref, qg_ref[...],
            wuqn_ref, wuqa_ref, wuqb_ref, wukt_ref, store)


def _q_sample(x, cos, sin, pre_g, wdq, qg, wuq_n, wuq_pa, wuq_pb, wuk_t):
    n = x.shape[0]
    return pl.pallas_call(
        _q_sample_kernel,
        out_shape=jax.ShapeDtypeStruct((N_HEADS, n, KCAT), BF16),
        compiler_params=pltpu.CompilerParams(vmem_limit_bytes=VMEM_LIMIT),
        name="q_sample",
    )(x, cos, sin, pre_g, wdq, qg, wuq_n, wuq_pa, wuq_pb, wuk_t)


def _decode_kernel(pt_ref, q_ref, knew_ref, ckv_hbm, kr_hbm, o_ref, cbuf, rbuf, sem):
    b = pl.program_id(0)
    nb = pl.num_programs(0)
    n_pages = cbuf.shape[1]

    def page_copies(seq, slot, p):
        page = pt_ref[seq, p]
        return (pltpu.make_async_copy(ckv_hbm.at[page], cbuf.at[slot, p], sem.at[0, slot]),
                pltpu.make_async_copy(kr_hbm.at[page], rbuf.at[slot, p], sem.at[1, slot]))

    def start_all(seq, slot):
        def body(p, carry):
            for cp in page_copies(seq, slot, p):
                cp.start()
            return carry
        lax.fori_loop(0, n_pages, body, 0)

    def wait_all(seq, slot):
        def body(p, carry):
            for cp in page_copies(seq, slot, p):
                cp.wait()
            return carry
        lax.fori_loop(0, n_pages, body, 0)

    slot = b % 2

    @pl.when(b == 0)
    def _():
        start_all(0, 0)

    @pl.when(b + 1 < nb)
    def _():
        start_all(b + 1, 1 - slot)

    wait_all(b, slot)

    q = q_ref[0]
    ql = q[:, :KV_LORA]
    qp = q[:, KV_LORA:KV_LORA + QK_ROPE]
    cp_rows = DEC_CHUNK_PAGES * PAGE_SIZE

    def chunk(c, carry):
        m_old, l_old, acc = carry
        p0 = pl.multiple_of(c * DEC_CHUNK_PAGES, DEC_CHUNK_PAGES)
        kc = cbuf[slot, pl.ds(p0, DEC_CHUNK_PAGES)].reshape(cp_rows, KV_LORA).astype(BF16)
        kr = rbuf[slot, pl.ds(p0, DEC_CHUNK_PAGES)].reshape(cp_rows, QK_ROPE).astype(BF16)
        s = _dot_nt(ql, kc) + _dot_nt(qp, kr)
        m_new = jnp.maximum(m_old, jnp.max(s, axis=-1, keepdims=True))
        alpha = jnp.exp(m_old - m_new)
        p = jnp.exp(s - m_new)
        l_new = l_old * alpha + jnp.sum(p, axis=-1, keepdims=True)
        acc = acc * alpha + _dot(p.astype(BF16), kc)
        return m_new, l_new, acc

    init = (jnp.full((N_HEADS, 1), -jnp.inf, F32), jnp.zeros((N_HEADS, 1), F32),
            jnp.zeros((N_HEADS, KV_LORA), F32))
    m_old, l_old, acc = lax.fori_loop(0, n_pages // DEC_CHUNK_PAGES, chunk, init)

    kn = knew_ref[0].astype(F32)
    s_new = jnp.sum(q.astype(F32) * kn, axis=-1, keepdims=True)
    m_new = jnp.maximum(m_old, s_new)
    alpha = jnp.exp(m_old - m_new)
    p_new = jnp.exp(s_new - m_new)
    l_new = l_old * alpha + p_new
    acc = acc * alpha + p_new * kn[:, :KV_LORA]
    o_ref[0] = acc / l_new


def _decode(page_table, q, knew, cache_ckv, cache_kr):
    n, n_pages = page_table.shape
    grid_spec = pltpu.PrefetchScalarGridSpec(
        num_scalar_prefetch=1,
        grid=(n,),
        in_specs=[
            pl.BlockSpec((1, N_HEADS, KCAT), lambda bi, pt: (bi, 0, 0)),
            pl.BlockSpec((1, 1, KCAT), lambda bi, pt: (bi, 0, 0)),
            pl.BlockSpec(memory_space=pl.ANY),
            pl.BlockSpec(memory_space=pl.ANY),
        ],
        out_specs=pl.BlockSpec((1, N_HEADS, KV_LORA), lambda bi, pt: (bi, 0, 0)),
        scratch_shapes=[
            pltpu.VMEM((2, n_pages, PAGE_SIZE, KV_LORA), F32),
            pltpu.VMEM((2, n_pages, PAGE_SIZE, QK_ROPE), F32),
            pltpu.SemaphoreType.DMA((2, 2)),
        ],
    )
    return pl.pallas_call(
        _decode_kernel,
        grid_spec=grid_spec,
        out_shape=jax.ShapeDtypeStruct((n, N_HEADS, KV_LORA), F32),
        compiler_params=_params(1),
        name="mla_decode",
    )(page_table, q, knew, cache_ckv, cache_kr)


def _out_sample_kernel(o_ref, x_ref, post_g_ref, wuv_ref, wo_ref, y_ref, os_ref):
    y_ref[...] = _mla_out(lambda hd: o_ref[hd], x_ref[...], post_g_ref[...], wuv_ref, wo_ref, os_ref)


def _out_sample(o_heads, x, post_g, wuv, wo):
    n, d = x.shape
    return pl.pallas_call(
        _out_sample_kernel,
        out_shape=jax.ShapeDtypeStruct((n, d), F32),
        scratch_shapes=[pltpu.VMEM((n, N_HEADS * V_HEAD), BF16)],
        compiler_params=pltpu.CompilerParams(vmem_limit_bytes=VMEM_LIMIT),
        name="out_sample",
    )(o_heads, x, post_g, wuv, wo)


def _rot_half_cols(w):
    half = QK_ROPE // 2
    return jnp.concatenate([-w[..., half:], w[..., :half]], axis=-1)


def _rope_tables(pos):
    half = QK_ROPE // 2
    inv = 1.0 / (ROPE_THETA ** (jnp.arange(half, dtype=F32) / half))
    ang = pos.astype(F32)[:, None] * inv[None, :]
    cos, sin = jnp.cos(ang), jnp.sin(ang)
    cos64 = jnp.concatenate([cos, cos], axis=-1)
    sin64 = jnp.concatenate([sin, sin], axis=-1)
    return cos64, sin64


def kernel(x_prompt, x_sample, cache_ckv, cache_kr, state_conv, page_table, pre_mix_g, post_mix_g,
           pre_ffn_g, post_ffn_g, w_in_a, sgu_g, w_s, b_s, w_out_a, kv_in_g, w_dkv, kv_g, w_uk, w_uv,
           w_dq, q_g, w_uq, w_o, w_up, conv_w, conv_b, w_down):
    depth = w_up.shape[0]
    n_a = w_in_a.shape[0]
    bp, tp, _ = x_prompt.shape
    ns = x_sample.shape[0]
    past_len = page_table.shape[1] * PAGE_SIZE
    row = lambda v: v.reshape(1, -1)

    w_in_b, w_out_b = w_in_a.astype(BF16), w_out_a.astype(BF16)
    w_up_b, w_down_b = w_up.astype(BF16), w_down.astype(BF16)
    w_dq_b, w_o_b = w_dq.astype(BF16), w_o.astype(BF16)
    wuq = w_uq.reshape(-1, Q_LORA, N_HEADS, QK_NOPE + QK_ROPE)
    wuq_n = wuq[..., :QK_NOPE].reshape(-1, Q_LORA, N_HEADS * QK_NOPE).astype(BF16)
    pad = lambda w: jnp.pad(w, ((0, 0),) * 3 + ((0, LANES - QK_ROPE),)).reshape(
        -1, Q_LORA, N_HEADS * LANES).astype(BF16)
    wuq_pa = pad(wuq[..., QK_NOPE:])
    wuq_pb = pad(_rot_half_cols(wuq[..., QK_NOPE:]))
    wuk_t = jnp.transpose(w_uk, (1, 2, 0)).astype(BF16)
    wuv_h = jnp.transpose(w_uv, (1, 0, 2)).astype(BF16)
    w_c = w_dkv[:, :KV_LORA].astype(BF16)
    w_pe = w_dkv[:, KV_LORA:]
    w_pe2 = jnp.concatenate([w_pe, _rot_half_cols(w_pe)], axis=-1).astype(BF16)
    bs_col = b_s[..., None]
    wdiag = jnp.repeat(w_s[:, :, 0, 0], SGU_GROUP_DIM, axis=-1)
    bdiag = jnp.repeat(b_s[:, :, 0], SGU_GROUP_DIM, axis=-1)

    cos_p, sin_p = _rope_tables(jnp.arange(tp, dtype=jnp.int32))
    cos_s, sin_s = _rope_tables(jnp.full((ns,), past_len, dtype=jnp.int32))
    tile2 = lambda a: jnp.concatenate([a, a], axis=-1)
    cs_p = jnp.concatenate([cos_p, sin_p], axis=-1)
    cs_s = jnp.concatenate([cos_s, sin_s], axis=-1)

    xp = x_prompt
    xs = x_sample.reshape(ns, D_MODEL)
    conv_p, conv_s, v_rows = [], [], []
    kcat_p = kcat_s = ckv_p = kr_p = ckv_s = kr_s = None
    for layer in range(depth):
        if layer == n_a:
            ckv_p, kr_p, kcat_p = _kv_side(xp, cs_p, row(kv_in_g), w_c, w_pe2, row(kv_g), TM_KV)
            ckv_s, kr_s, kcat_s = _kv_side(xs[None], cs_s, row(kv_in_g), w_c, w_pe2, row(kv_g), ns)
        pre_g, post_g = row(pre_mix_g[layer]), row(post_mix_g[layer])
        if layer < n_a:
            xp = _sgu_prompt(xp, pre_g, post_g, w_in_b[layer], row(sgu_g[layer]), w_s[layer],
                             bs_col[layer], w_out_b[layer])
            xs, v = _sgu_sample(xs, pre_g, post_g, w_in_b[layer], row(sgu_g[layer]), row(wdiag[layer]),
                                row(bdiag[layer]), w_out_b[layer])
            v_rows.append(v)
        else:
            j = layer - n_a
            qw = (w_dq_b[j], row(q_g[j]), wuq_n[j], wuq_pa[j], wuq_pb[j], wuk_t)
            xp = _mla_prompt(xp, kcat_p, tile2(cos_p), tile2(sin_p), pre_g, post_g, *qw, wuv_h, w_o_b[j])
            q = _q_sample(xs, tile2(cos_s), tile2(sin_s), pre_g, *qw)
            o = _decode(page_table, jnp.transpose(q, (1, 0, 2)), kcat_s.reshape(ns, 1, KCAT),
                        cache_ckv, cache_kr)
            xs = _out_sample(jnp.transpose(o, (1, 0, 2)), xs, post_g, wuv_h, w_o_b[j])
        pre_g, post_g = row(pre_ffn_g[layer]), row(post_ffn_g[layer])
        xp, tail = _ffn_prompt(xp, pre_g, post_g, w_up_b[layer], conv_w[layer], row(conv_b[layer]),
                               w_down_b[layer])
        conv_p.append(tail)
        xs, st = _ffn_sample(xs, state_conv[layer], pre_g, post_g, w_up_b[layer], conv_w[layer],
                             row(conv_b[layer]), w_down_b[layer])
        conv_s.append(st)

    return (xp, xs.reshape(ns, 1, D_MODEL), ckv_p, kr_p, jnp.stack(conv_p),
            ckv_s.reshape(ns, 1, KV_LORA), kr_s.reshape(ns, 1, QK_ROPE), jnp.stack(conv_s),
            jnp.stack(v_rows).reshape(n_a, ns, 1, SGU_WIDTH))
```

```python
import jax
import jax.numpy as jnp
from jax import lax
from jax.experimental import pallas as pl
from jax.experimental.pallas import tpu as pltpu

F32 = jnp.float32
BF16 = jnp.bfloat16

D_MODEL = 1024
N_A_LAYERS = 2
CHUNK = 128
SGU_WIDTH = 2 * D_MODEL
SGU_GROUPS = 8
SGU_GROUP_DIM = SGU_WIDTH // SGU_GROUPS
N_HEADS = 8
QK_NOPE = 128
QK_ROPE = 64
V_HEAD = 128
Q_LORA = D_MODEL // 2
KV_LORA = D_MODEL // 4
ROPE_THETA = 10000.0
ATTN_SCALE = (QK_NOPE + QK_ROPE) ** -0.5
D_FF = 11 * D_MODEL // 4
CONV_W = 3
EPS = 1e-6
PAGE_SIZE = 128

LANES = 128
BF16_ROWS = 16
VMEM_LIMIT = 56 * 1024 * 1024

KCAT = KV_LORA + LANES
FF_TILE = 256
N_FF = D_FF // FF_TILE
TM_FFN = 512
TM_SGU = 512
TQ = 256
DEC_CHUNK_PAGES = 8


def _rms(x, g):
    return x * lax.rsqrt(jnp.mean(x * x, axis=-1, keepdims=True) + EPS) * g


def _gelu_erf(x):
    return 0.5 * x * (1.0 + lax.erf(x * (0.5 ** 0.5)))


def _dot(a, b):
    return jnp.dot(a, b, preferred_element_type=F32)


def _dot_nt(a, b):
    return lax.dot_general(a, b, (((1,), (1,)), ((), ())), preferred_element_type=F32)


def _params(n_grid_axes):
    return pltpu.CompilerParams(
        dimension_semantics=("arbitrary",) * n_grid_axes, vmem_limit_bytes=VMEM_LIMIT)


def _resident(shape):
    nd = len(shape)
    return pl.BlockSpec(shape, lambda *_: (0,) * nd, pipeline_mode=pl.Buffered(1))


def _ffn_prompt_kernel(x_ref, halo_ref, pre_g_ref, post_g_ref, wup_ref, cw_ref, cb_ref, wdn_ref,
                       y_ref, tail_ref, act_ref, h_ref):
    tm = x_ref.shape[1]
    i = pl.program_id(1)
    x = x_ref[0]

    @pl.when(i >= 0)
    def _():
        g = pre_g_ref[...]
        h_ref[:BF16_ROWS] = jnp.where(i > 0, _rms(halo_ref[0], g), 0.0).astype(BF16)
        h_ref[BF16_ROWS:] = _rms(x, g).astype(BF16)

    def conv(col0):
        sl = slice(col0, col0 + FF_TILE)
        a = _dot(h_ref[...], wup_ref[:, sl])
        w = cw_ref[:, sl]
        c = cb_ref[:, sl] + pltpu.roll(a, 2, 0)[BF16_ROWS:] * w[0:1]
        c = c + pltpu.roll(a, 1, 0)[BF16_ROWS:] * w[1:2]
        c = c + a[BF16_ROWS:] * w[2:3]
        tail_ref[0, 0, :, sl] = a[tm + 8:]
        return c

    for f in range(N_FF):
        cg = conv(f * FF_TILE)
        cu = conv(D_FF + f * FF_TILE)
        act_ref[:, f * FF_TILE:(f + 1) * FF_TILE] = (jax.nn.gelu(cg, approximate=True) * cu).astype(BF16)
    y_ref[0] = x + _rms(_dot(act_ref[...], wdn_ref[...]), post_g_ref[...])


def _ffn_prompt(x, pre_g, post_g, wup, cw, cb, wdn):
    b, t, d = x.shape
    tm = TM_FFN
    nt = t // tm
    halo_blocks = tm // BF16_ROWS
    y, tail = pl.pallas_call(
        _ffn_prompt_kernel,
        grid=(b, nt),
        in_specs=[
            pl.BlockSpec((1, tm, d), lambda bi, i: (bi, i, 0)),
            pl.BlockSpec((1, BF16_ROWS, d), lambda bi, i: (bi, jnp.maximum(i * halo_blocks - 1, 0), 0)),
            _resident((1, d)), _resident((1, d)),
            _resident(wup.shape), _resident(cw.shape), _resident(cb.shape), _resident(wdn.shape),
        ],
        out_specs=[
            pl.BlockSpec((1, tm, d), lambda bi, i: (bi, i, 0)),
            pl.BlockSpec((1, 1, 8, 2 * D_FF), lambda bi, i: (bi, i, 0, 0)),
        ],
        out_shape=[
            jax.ShapeDtypeStruct((b, t, d), F32),
            jax.ShapeDtypeStruct((b, nt, 8, 2 * D_FF), F32),
        ],
        scratch_shapes=[pltpu.VMEM((tm, D_FF), BF16), pltpu.VMEM((BF16_ROWS + tm, d), BF16)],
        compiler_params=_params(2),
        name="ffn_prompt",
    )(x, x, pre_g, post_g, wup, cw, cb, wdn)
    return y, tail[:, nt - 1, 8 - (CONV_W - 1):, :]


def _ffn_sample_kernel(x_ref, p0g_ref, p0u_ref, p1g_ref, p1u_ref, pre_g_ref, post_g_ref,
                       wg_ref, wu_ref, cwg_ref, cwu_ref, cbg_ref, cbu_ref, wdn_ref,
                       y_ref, ag_ref, au_ref, h_ref, acc_ref):
    f = pl.program_id(0)

    @pl.when(f == 0)
    def _():
        h_ref[...] = _rms(x_ref[...], pre_g_ref[...]).astype(BF16)

    h = h_ref[...]

    def conv(w_ref, p0_ref, p1_ref, cw_ref, cb_ref, a_ref):
        a = _dot(h, w_ref[...])
        a_ref[...] = a
        w = cw_ref[...]
        c = cb_ref[...] + p0_ref[...] * w[0:1]
        c = c + p1_ref[...] * w[1:2]
        return c + a * w[2:3]

    cg = conv(wg_ref, p0g_ref, p1g_ref, cwg_ref, cbg_ref, ag_ref)
    cu = conv(wu_ref, p0u_ref, p1u_ref, cwu_ref, cbu_ref, au_ref)
    act = (jax.nn.gelu(cg, approximate=True) * cu).astype(BF16)
    part = _dot(act, wdn_ref[...])

    @pl.when(f == 0)
    def _():
        acc_ref[...] = part

    @pl.when(f > 0)
    def _():
        acc_ref[...] += part

    @pl.when(f == pl.num_programs(0) - 1)
    def _():
        y_ref[...] = x_ref[...] + _rms(acc_ref[...], post_g_ref[...])


def _ffn_sample(x, state, pre_g, post_g, wup, cw, cb, wdn):
    n, d = x.shape
    tf = FF_TILE
    st = state.reshape(n, (CONV_W - 1) * 2 * D_FF)
    hist = 2 * D_FF // tf
    col = lambda off: pl.BlockSpec((n, tf), lambda f: (0, off + f))
    row3 = lambda off: pl.BlockSpec((CONV_W, tf), lambda f: (0, off + f))
    row1 = lambda off: pl.BlockSpec((1, tf), lambda f: (0, off + f))
    const = lambda shape: pl.BlockSpec(shape, lambda f: (0,) * len(shape))
    y, ag, au = pl.pallas_call(
        _ffn_sample_kernel,
        grid=(N_FF,),
        in_specs=[
            const((n, d)),
            col(0), col(N_FF), col(hist), col(hist + N_FF),
            const((1, d)), const((1, d)),
            pl.BlockSpec((d, tf), lambda f: (0, f)), pl.BlockSpec((d, tf), lambda f: (0, N_FF + f)),
            row3(0), row3(N_FF), row1(0), row1(N_FF),
            pl.BlockSpec((tf, d), lambda f: (f, 0)),
        ],
        out_specs=[const((n, d)), col(0), col(0)],
        out_shape=[
            jax.ShapeDtypeStruct((n, d), F32),
            jax.ShapeDtypeStruct((n, D_FF), F32),
            jax.ShapeDtypeStruct((n, D_FF), F32),
        ],
        scratch_shapes=[pltpu.VMEM((n, d), BF16), pltpu.VMEM((n, d), F32)],
        compiler_params=_params(1),
        name="ffn_sample",
    )(x, st, st, st, st, pre_g, post_g, wup, wup, cw, cw, cb, cb, wdn)
    new_state = jnp.stack([state[:, 1, :], jnp.concatenate([ag, au], axis=-1)], axis=1)
    return y, new_state


def _sgu_prompt_kernel(x_ref, pre_g_ref, post_g_ref, win_ref, sgug_ref, ws_ref, bs_ref, wout_ref,
                       y_ref, v_ref, acc_ref):
    tm = x_ref.shape[1]
    gd = SGU_GROUP_DIM
    x = x_ref[0]
    h = _rms(x, pre_g_ref[...]).astype(BF16)
    ss = jnp.zeros((tm, 1), F32)
    for j in range(SGU_GROUPS):
        zc = _gelu_erf(_dot(h, win_ref[:, SGU_WIDTH + j * gd:SGU_WIDTH + (j + 1) * gd]))
        v_ref[:, j * gd:(j + 1) * gd] = zc
        ss = ss + jnp.sum(zc * zc, axis=-1, keepdims=True)
    inv = lax.rsqrt(ss * (1.0 / SGU_WIDTH) + EPS)
    causal = (lax.broadcasted_iota(jnp.int32, (CHUNK, CHUNK), 0)
              >= lax.broadcasted_iota(jnp.int32, (CHUNK, CHUNK), 1))
    for g in range(SGU_GROUPS):
        sl = slice(g * gd, (g + 1) * gd)
        vn = (v_ref[:, sl] * inv * sgug_ref[:, sl]).astype(BF16)
        wm = jnp.where(causal, ws_ref[g], 0.0).astype(BF16)
        bias = bs_ref[g]
        s = jnp.concatenate(
            [_dot(wm, vn[c * CHUNK:(c + 1) * CHUNK]) + bias for c in range(tm // CHUNK)], axis=0)
        u = _gelu_erf(_dot(h, win_ref[:, sl]))
        part = _dot((u * s).astype(BF16), wout_ref[sl, :])
        if g == 0:
            acc_ref[...] = part
        else:
            acc_ref[...] += part
    y_ref[0] = x + _rms(acc_ref[...], post_g_ref[...])


def _sgu_prompt(x, pre_g, post_g, win, sgug, ws, bs, wout):
    b, t, d = x.shape
    tm = TM_SGU
    return pl.pallas_call(
        _sgu_prompt_kernel,
        grid=(b, t // tm),
        in_specs=[
            pl.BlockSpec((1, tm, d), lambda bi, i: (bi, i, 0)),
            _resident((1, d)), _resident((1, d)),
            _resident(win.shape), _resident(sgug.shape), _resident(ws.shape), _resident(bs.shape),
            _resident(wout.shape),
        ],
        out_specs=pl.BlockSpec((1, tm, d), lambda bi, i: (bi, i, 0)),
        out_shape=jax.ShapeDtypeStruct((b, t, d), F32),
        scratch_shapes=[pltpu.VMEM((tm, SGU_WIDTH), F32), pltpu.VMEM((tm, d), F32)],
        compiler_params=_params(2),
        name="sgu_prompt",
    )(x, pre_g, post_g, win, sgug, ws, bs, wout)


def _sgu_sample_kernel(x_ref, pre_g_ref, post_g_ref, win_ref, sgug_ref, wdiag_ref, bdiag_ref, wout_ref,
                       y_ref, v_ref):
    x = x_ref[...]
    h = _rms(x, pre_g_ref[...]).astype(BF16)
    u = _gelu_erf(_dot(h, win_ref[:, :SGU_WIDTH]))
    v = _gelu_erf(_dot(h, win_ref[:, SGU_WIDTH:]))
    vn = _rms(v, sgug_ref[...])
    v_ref[...] = vn
    s = vn * wdiag_ref[...] + bdiag_ref[...]
    y = _dot((u * s).astype(BF16), wout_ref[...])
    y_ref[...] = x + _rms(y, post_g_ref[...])


def _sgu_sample(x, pre_g, post_g, win, sgug, wdiag, bdiag, wout):
    n, d = x.shape
    return pl.pallas_call(
        _sgu_sample_kernel,
        out_shape=[jax.ShapeDtypeStruct((n, d), F32), jax.ShapeDtypeStruct((n, SGU_WIDTH), F32)],
        compiler_params=pltpu.CompilerParams(vmem_limit_bytes=VMEM_LIMIT),
        name="sgu_sample",
    )(x, pre_g, post_g, win, sgug, wdiag, bdiag, wout)


def _kv_side_kernel(x_ref, cs_ref, g_in_ref, wc_ref, wpe_ref, kvg_ref, ckv_ref, kr_ref, kcat_ref, ckvt_ref):
    h = _rms(x_ref[0], g_in_ref[...]).astype(BF16)
    ckv = _rms(_dot(h, wc_ref[...]), kvg_ref[...])
    t = _dot(h, wpe_ref[...]) * cs_ref[...]
    t = t + pltpu.roll(t, QK_ROPE, 1)
    lane = lax.broadcasted_iota(jnp.int32, t.shape, 1)
    krp = jnp.where(lane < QK_ROPE, t, 0.0)
    ckv_ref[0] = ckv
    kr_ref[0] = t[:, :QK_ROPE]
    kcat_ref[0] = jnp.concatenate([ckv, krp], axis=1).astype(BF16)
    ckvt_ref[0, 0] = ckv.T.astype(BF16)


def _kv_side(x, cs_tab, g_in, wc, wpe, kvg, tm):
    b, t, d = x.shape
    return pl.pallas_call(
        _kv_side_kernel,
        grid=(b, t // tm),
        in_specs=[
            pl.BlockSpec((1, tm, d), lambda bi, i: (bi, i, 0)),
            pl.BlockSpec((tm, LANES), lambda bi, i: (i, 0)),
            _resident((1, d)), _resident(wc.shape), _resident(wpe.shape), _resident((1, KV_LORA)),
        ],
        out_specs=[
            pl.BlockSpec((1, tm, KV_LORA), lambda bi, i: (bi, i, 0)),
            pl.BlockSpec((1, tm, QK_ROPE), lambda bi, i: (bi, i, 0)),
            pl.BlockSpec((1, tm, KCAT), lambda bi, i: (bi, i, 0)),
            pl.BlockSpec((1, 1, KV_LORA, tm), lambda bi, i: (bi, i, 0, 0)),
        ],
        out_shape=[
            jax.ShapeDtypeStruct((b, t, KV_LORA), F32),
            jax.ShapeDtypeStruct((b, t, QK_ROPE), F32),
            jax.ShapeDtypeStruct((b, t, KCAT), BF16),
            jax.ShapeDtypeStruct((b, t // tm, KV_LORA, tm), BF16),
        ],
        compiler_params=_params(2),
        name="kv_side",
    )(x, cs_tab, g_in, wc, wpe, kvg)


def _q_rows(x, cos, sin, pre_g, wdq, qg, wuq_n, wuq_pa, wuq_pb, wuk_t, store):
    h = _rms(x, pre_g).astype(BF16)
    cq = _rms(_dot(h, wdq[...]), qg).astype(BF16)
    qn = _dot(cq, wuq_n[...])
    qa = _dot(cq, wuq_pa[...])
    qb = _dot(cq, wuq_pb[...])
    for hd in range(N_HEADS):
        sl = slice(hd * LANES, (hd + 1) * LANES)
        qpe = qa[:, sl] * cos + qb[:, sl] * sin
        ql = _dot(qn[:, sl].astype(BF16), wuk_t[hd])
        store(hd, (ql * ATTN_SCALE).astype(BF16), (qpe * ATTN_SCALE).astype(BF16))


def _mla_out(o_heads, x, post_g, wuv, wo, o_ref):
    for hd in range(N_HEADS):
        o_ref[:, hd * V_HEAD:(hd + 1) * V_HEAD] = _dot(o_heads(hd).astype(BF16), wuv[hd]).astype(BF16)
    m = _dot(o_ref[...], wo[...])
    return x + _rms(m, post_g)


def _mla_prompt_kernel(x_ref, k_ref, vt_ref, cos_ref, sin_ref, pre_g_ref, post_g_ref, wdq_ref, qg_ref,
                       wuqn_ref, wuqa_ref, wuqb_ref, wuk_ref, wuvt_ref, wot_ref,
                       y_ref, qt_ref, m_ref, l_ref, acc_ref, o_ref):
    tq = x_ref.shape[1]
    qi = pl.program_id(1)
    x = x_ref[0]

    h = _rms(x, pre_g_ref[...]).astype(BF16)
    cq = _rms(_dot(h, wdq_ref[...]), qg_ref[...])
    cq_t = cq.T.astype(BF16)
    qn_t = _dot(wuqn_ref[...], cq_t)
    qa_t = _dot(wuqa_ref[...], cq_t)
    qb_t = _dot(wuqb_ref[...], cq_t)
    cos_t, sin_t = cos_ref[...], sin_ref[...]
    for hd in range(N_HEADS):
        cols = slice(hd * tq, (hd + 1) * tq)
        ql_t = _dot(wuk_ref[hd], qn_t[hd * QK_NOPE:(hd + 1) * QK_NOPE].astype(BF16))
        rows = slice(hd * QK_ROPE, (hd + 1) * QK_ROPE)
        qpe_t = qa_t[rows] * cos_t + qb_t[rows] * sin_t
        qt_ref[:KV_LORA, cols] = (ql_t * ATTN_SCALE).astype(BF16)
        qt_ref[KV_LORA:KV_LORA + QK_ROPE, cols] = (qpe_t * ATTN_SCALE).astype(BF16)
    qt_ref[KV_LORA + QK_ROPE:, :] = jnp.zeros((KCAT - KV_LORA - QK_ROPE, N_HEADS * tq), BF16)

    m_ref[...] = jnp.full(m_ref.shape, -jnp.inf, F32)
    l_ref[...] = jnp.zeros(l_ref.shape, F32)
    acc_ref[...] = jnp.zeros(acc_ref.shape, F32)

    def step(kb, diagonal):
        k = k_ref[0, pl.ds(pl.multiple_of(kb * tq, tq), tq), :]
        vt = vt_ref[0, kb]
        scores = [_dot(k, qt_ref[:, hd * tq:(hd + 1) * tq]) for hd in range(N_HEADS)]
        for hd in range(N_HEADS):
            cols = slice(hd * tq, (hd + 1) * tq)
            st = scores[hd]
            if diagonal:
                key = lax.broadcasted_iota(jnp.int32, st.shape, 0)
                qry = lax.broadcasted_iota(jnp.int32, st.shape, 1)
                st = jnp.where(key <= qry, st, -jnp.inf)
            m_old = m_ref[hd:hd + 1, :]
            m_new = jnp.maximum(m_old, jnp.max(st, axis=0, keepdims=True))
            alpha = jnp.exp(m_old - m_new)
            p = jnp.exp(st - m_new)
            l_ref[hd:hd + 1, :] = l_ref[hd:hd + 1, :] * alpha + jnp.sum(p, axis=0, keepdims=True)
            acc_ref[:, cols] = acc_ref[:, cols] * alpha + _dot(vt, p.astype(BF16))
            m_ref[hd:hd + 1, :] = m_new

    def body(kb, carry):
        step(kb, False)
        return carry

    lax.fori_loop(0, qi, body, 0)
    step(qi, True)

    for hd in range(N_HEADS):
        cols = slice(hd * tq, (hd + 1) * tq)
        o_t = acc_ref[:, cols] * (1.0 / l_ref[hd:hd + 1, :])
        o_ref[hd * V_HEAD:(hd + 1) * V_HEAD, :] = _dot(wuvt_ref[hd], o_t.astype(BF16)).astype(BF16)
    m_t = _dot(wot_ref[...], o_ref[...])
    y_ref[0] = x + _rms(m_t.T, post_g_ref[...])


def _mla_prompt(x, kcat, ckv_t, cos_t, sin_t, pre_g, post_g, wdq, qg, wuqn_t, wuqa_t, wuqb_t, wuk_h,
                wuvt_h, wo_t):
    b, t, d = x.shape
    tq = TQ
    lanes = N_HEADS * tq
    weights = (wdq, qg, wuqn_t, wuqa_t, wuqb_t, wuk_h, wuvt_h, wo_t)
    return pl.pallas_call(
        _mla_prompt_kernel,
        grid=(b, t // tq),
        in_specs=[
            pl.BlockSpec((1, tq, d), lambda bi, i: (bi, i, 0)),
            pl.BlockSpec((1, t, KCAT), lambda bi, i: (bi, 0, 0)),
            pl.BlockSpec((1, t // tq, KV_LORA, tq), lambda bi, i: (bi, 0, 0, 0)),
            pl.BlockSpec((QK_ROPE, tq), lambda bi, i: (0, i)),
            pl.BlockSpec((QK_ROPE, tq), lambda bi, i: (0, i)),
            _resident((1, d)), _resident((1, d)),
        ] + [_resident(w.shape) for w in weights],
        out_specs=pl.BlockSpec((1, tq, d), lambda bi, i: (bi, i, 0)),
        out_shape=jax.ShapeDtypeStruct((b, t, d), F32),
        scratch_shapes=[
            pltpu.VMEM((KCAT, lanes), BF16),
            pltpu.VMEM((N_HEADS, tq), F32), pltpu.VMEM((N_HEADS, tq), F32),
            pltpu.VMEM((KV_LORA, lanes), F32),
            pltpu.VMEM((N_HEADS * V_HEAD, tq), BF16),
        ],
        compiler_params=_params(2),
        name="mla_prompt",
    )(x, kcat, ckv_t, cos_t, sin_t, pre_g, post_g, *weights)


def _q_sample_kernel(x_ref, cos_ref, sin_ref, pre_g_ref, wdq_ref, qg_ref, wuqn_ref, wuqa_ref, wuqb_ref,
                     wukt_ref, q_ref):
    def store(hd, ql, qpe):
        q_ref[hd, :, :KV_LORA] = ql
        q_ref[hd, :, KV_LORA:] = qpe

    _q_rows(x_ref[...], cos_ref[...], sin_ref[...], pre_g_ref[...], wdq_ref, qg_ref[...],
            wuqn_ref, wuqa_ref, wuqb_ref, wukt_ref, store)


def _q_sample(x, cos, sin, pre_g, wdq, qg, wuq_n, wuq_pa, wuq_pb, wuk_t):
    n = x.shape[0]
    return pl.pallas_call(
        _q_sample_kernel,
        out_shape=jax.ShapeDtypeStruct((N_HEADS, n, KCAT), BF16),
        compiler_params=pltpu.CompilerParams(vmem_limit_bytes=VMEM_LIMIT),
        name="q_sample",
    )(x, cos, sin, pre_g, wdq, qg, wuq_n, wuq_pa, wuq_pb, wuk_t)


def _decode_kernel(pt_ref, q_ref, knew_ref, ckv_hbm, krt_hbm, o_ref, cbuf, rbuf, sem):
    b = pl.program_id(0)
    nb = pl.num_programs(0)
    n_pages = cbuf.shape[1]
    n_chunks = n_pages // DEC_CHUNK_PAGES
    cp_rows = DEC_CHUNK_PAGES * PAGE_SIZE

    def page_copies(page, slot, p):
        return (pltpu.make_async_copy(ckv_hbm.at[page], cbuf.at[slot, p], sem.at[0, slot]),
                pltpu.make_async_copy(krt_hbm.at[page], rbuf.at[slot, p], sem.at[1, slot]))

    def start_pages(seq, slot, p_lo, p_hi):
        for p in range(p_lo, p_hi):
            for cp in page_copies(pt_ref[seq, p], slot, p):
                cp.start()

    def wait_slot(slot):
        for p in range(n_pages):
            for cp in page_copies(0, slot, p):
                cp.wait()

    slot = b % 2
    nxt = jnp.minimum(b + 1, nb - 1)

    @pl.when(b == 0)
    def _():
        start_pages(0, 0, 0, n_pages)

    wait_slot(slot)

    q = q_ref[0]
    ql = q[:, :KV_LORA]
    qp = q[:, KV_LORA:KV_LORA + QK_ROPE]

    def latent(c):
        rows = cbuf[slot, c * DEC_CHUNK_PAGES:(c + 1) * DEC_CHUNK_PAGES]
        return rows.reshape(cp_rows, KV_LORA).astype(BF16)

    scores = []
    for c in range(n_chunks):
        start_pages(nxt, 1 - slot, c * DEC_CHUNK_PAGES, (c + 1) * DEC_CHUNK_PAGES)
        s_pe = jnp.concatenate(
            [_dot(qp, rbuf[slot, c * DEC_CHUNK_PAGES + j].astype(BF16)) for j in range(DEC_CHUNK_PAGES)],
            axis=1)
        scores.append(_dot_nt(ql, latent(c)) + s_pe)

    kn = knew_ref[0].astype(F32)
    s_new = jnp.sum(q.astype(F32) * kn, axis=-1, keepdims=True)
    m = s_new
    for s in scores:
        m = jnp.maximum(m, jnp.max(s, axis=-1, keepdims=True))
    p_new = jnp.exp(s_new - m)
    l = p_new
    acc = p_new * kn[:, :KV_LORA]
    for c, s in enumerate(scores):
        p = jnp.exp(s - m)
        l = l + jnp.sum(p, axis=-1, keepdims=True)
        acc = acc + _dot(p.astype(BF16), latent(c))
    o_ref[0] = acc / l

    @pl.when(b == nb - 1)
    def _():
        wait_slot(1 - slot)


def _decode(page_table, q, knew, cache_ckv, cache_krt):
    n, n_pages = page_table.shape
    grid_spec = pltpu.PrefetchScalarGridSpec(
        num_scalar_prefetch=1,
        grid=(n,),
        in_specs=[
            pl.BlockSpec((1, N_HEADS, KCAT), lambda bi, pt: (bi, 0, 0)),
            pl.BlockSpec((1, 1, KCAT), lambda bi, pt: (bi, 0, 0)),
            pl.BlockSpec(memory_space=pl.ANY),
            pl.BlockSpec(memory_space=pl.ANY),
        ],
        out_specs=pl.BlockSpec((1, N_HEADS, KV_LORA), lambda bi, pt: (bi, 0, 0)),
        scratch_shapes=[
            pltpu.VMEM((2, n_pages, PAGE_SIZE, KV_LORA), F32),
            pltpu.VMEM((2, n_pages, QK_ROPE, PAGE_SIZE), F32),
            pltpu.SemaphoreType.DMA((2, 2)),
        ],
    )
    return pl.pallas_call(
        _decode_kernel,
        grid_spec=grid_spec,
        out_shape=jax.ShapeDtypeStruct((n, N_HEADS, KV_LORA), F32),
        compiler_params=_params(1),
        name="mla_decode",
    )(page_table, q, knew, cache_ckv, cache_krt)


def _out_sample_kernel(o_ref, x_ref, post_g_ref, wuv_ref, wo_ref, y_ref, os_ref):
    y_ref[...] = _mla_out(lambda hd: o_ref[hd], x_ref[...], post_g_ref[...], wuv_ref, wo_ref, os_ref)


def _out_sample(o_heads, x, post_g, wuv, wo):
    n, d = x.shape
    return pl.pallas_call(
        _out_sample_kernel,
        out_shape=jax.ShapeDtypeStruct((n, d), F32),
        scratch_shapes=[pltpu.VMEM((n, N_HEADS * V_HEAD), BF16)],
        compiler_params=pltpu.CompilerParams(vmem_limit_bytes=VMEM_LIMIT),
        name="out_sample",
    )(o_heads, x, post_g, wuv, wo)


def _rot_half_cols(w):
    half = QK_ROPE // 2
    return jnp.concatenate([-w[..., half:], w[..., :half]], axis=-1)


def _rope_tables(pos):
    half = QK_ROPE // 2
    inv = 1.0 / (ROPE_THETA ** (jnp.arange(half, dtype=F32) / half))
    ang = pos.astype(F32)[:, None] * inv[None, :]
    cos, sin = jnp.cos(ang), jnp.sin(ang)
    cos64 = jnp.concatenate([cos, cos], axis=-1)
    sin64 = jnp.concatenate([sin, sin], axis=-1)
    return cos64, sin64


def kernel(x_prompt, x_sample, cache_ckv, cache_kr, state_conv, page_table, pre_mix_g, post_mix_g,
           pre_ffn_g, post_ffn_g, w_in_a, sgu_g, w_s, b_s, w_out_a, kv_in_g, w_dkv, kv_g, w_uk, w_uv,
           w_dq, q_g, w_uq, w_o, w_up, conv_w, conv_b, w_down):
    depth = w_up.shape[0]
    n_a = w_in_a.shape[0]
    bp, tp, _ = x_prompt.shape
    ns = x_sample.shape[0]
    past_len = page_table.shape[1] * PAGE_SIZE
    row = lambda v: v.reshape(1, -1)

    w_in_b, w_out_b = w_in_a.astype(BF16), w_out_a.astype(BF16)
    w_up_b, w_down_b = w_up.astype(BF16), w_down.astype(BF16)
    w_dq_b, w_o_b = w_dq.astype(BF16), w_o.astype(BF16)
    wuq = w_uq.reshape(-1, Q_LORA, N_HEADS, QK_NOPE + QK_ROPE)
    wuq_n = wuq[..., :QK_NOPE].reshape(-1, Q_LORA, N_HEADS * QK_NOPE).astype(BF16)
    pad = lambda w: jnp.pad(w, ((0, 0),) * 3 + ((0, LANES - QK_ROPE),)).reshape(
        -1, Q_LORA, N_HEADS * LANES).astype(BF16)
    wuq_pa = pad(wuq[..., QK_NOPE:])
    wuq_pb = pad(_rot_half_cols(wuq[..., QK_NOPE:]))
    wuk_t = jnp.transpose(w_uk, (1, 2, 0)).astype(BF16)
    wuv_h = jnp.transpose(w_uv, (1, 0, 2)).astype(BF16)
    wuqn_t = jnp.swapaxes(wuq_n, 1, 2)
    flat_t = lambda w: jnp.swapaxes(w.reshape(-1, Q_LORA, N_HEADS * QK_ROPE), 1, 2).astype(BF16)
    wuqa_t = flat_t(wuq[..., QK_NOPE:])
    wuqb_t = flat_t(_rot_half_cols(wuq[..., QK_NOPE:]))
    wuk_h = jnp.transpose(w_uk, (1, 0, 2)).astype(BF16)
    wuvt_h = jnp.transpose(w_uv, (1, 2, 0)).astype(BF16)
    wo_t = jnp.swapaxes(w_o_b, 1, 2)
    cache_krt = jnp.swapaxes(cache_kr, 1, 2)
    w_c = w_dkv[:, :KV_LORA].astype(BF16)
    w_pe = w_dkv[:, KV_LORA:]
    w_pe2 = jnp.concatenate([w_pe, _rot_half_cols(w_pe)], axis=-1).astype(BF16)
    bs_col = b_s[..., None]
    wdiag = jnp.repeat(w_s[:, :, 0, 0], SGU_GROUP_DIM, axis=-1)
    bdiag = jnp.repeat(b_s[:, :, 0], SGU_GROUP_DIM, axis=-1)

    cos_p, sin_p = _rope_tables(jnp.arange(tp, dtype=jnp.int32))
    cos_s, sin_s = _rope_tables(jnp.full((ns,), past_len, dtype=jnp.int32))
    tile2 = lambda a: jnp.concatenate([a, a], axis=-1)
    cs_p = jnp.concatenate([cos_p, sin_p], axis=-1)
    cs_s = jnp.concatenate([cos_s, sin_s], axis=-1)

    xp = x_prompt
    xs = x_sample.reshape(ns, D_MODEL)
    conv_p, conv_s, v_rows = [], [], []
    kcat_p = kcat_s = ckvt_p = ckv_p = kr_p = ckv_s = kr_s = None
    for layer in range(depth):
        if layer == n_a:
            ckv_p, kr_p, kcat_p, ckvt_p = _kv_side(xp, cs_p, row(kv_in_g), w_c, w_pe2, row(kv_g), TQ)
            ckv_s, kr_s, kcat_s, _ = _kv_side(xs[None], cs_s, row(kv_in_g), w_c, w_pe2, row(kv_g), ns)
        pre_g, post_g = row(pre_mix_g[layer]), row(post_mix_g[layer])
        if layer < n_a:
            xp = _sgu_prompt(xp, pre_g, post_g, w_in_b[layer], row(sgu_g[layer]), w_s[layer],
                             bs_col[layer], w_out_b[layer])
            xs, v = _sgu_sample(xs, pre_g, post_g, w_in_b[layer], row(sgu_g[layer]), row(wdiag[layer]),
                                row(bdiag[layer]), w_out_b[layer])
            v_rows.append(v)
        else:
            j = layer - n_a
            qw = (w_dq_b[j], row(q_g[j]), wuq_n[j], wuq_pa[j], wuq_pb[j], wuk_t)
            xp = _mla_prompt(xp, kcat_p, ckvt_p, cos_p.T, sin_p.T, pre_g, post_g, w_dq_b[j], row(q_g[j]),
                             wuqn_t[j], wuqa_t[j], wuqb_t[j], wuk_h, wuvt_h, wo_t[j])
            q = _q_sample(xs, tile2(cos_s), tile2(sin_s), pre_g, *qw)
            o = _decode(page_table, jnp.transpose(q, (1, 0, 2)), kcat_s.reshape(ns, 1, KCAT),
                        cache_ckv, cache_krt)
            xs = _out_sample(jnp.transpose(o, (1, 0, 2)), xs, post_g, wuv_h, w_o_b[j])
        pre_g, post_g = row(pre_ffn_g[layer]), row(post_ffn_g[layer])
        xp, tail = _ffn_prompt(xp, pre_g, post_g, w_up_b[layer], conv_w[layer], row(conv_b[layer]),
                               w_down_b[layer])
        conv_p.append(tail)
        xs, st = _ffn_sample(xs, state_conv[layer], pre_g, post_g, w_up_b[layer], conv_w[layer],
                             row(conv_b[layer]), w_down_b[layer])
        conv_s.append(st)

    return (xp, xs.reshape(ns, 1, D_MODEL), ckv_p, kr_p, jnp.stack(conv_p),
            ckv_s.reshape(ns, 1, KV_LORA), kr_s.reshape(ns, 1, QK_ROPE), jnp.stack(conv_s),
            jnp.stack(v_rows).reshape(n_a, ns, 1, SGU_WIDTH))
```

```python
import jax
import jax.numpy as jnp
from jax import lax
from jax.experimental import pallas as pl
from jax.experimental.pallas import tpu as pltpu

F32 = jnp.float32
BF16 = jnp.bfloat16

D_MODEL = 1024
N_A_LAYERS = 2
CHUNK = 128
SGU_WIDTH = 2 * D_MODEL
SGU_GROUPS = 8
SGU_GROUP_DIM = SGU_WIDTH // SGU_GROUPS
N_HEADS = 8
QK_NOPE = 128
QK_ROPE = 64
V_HEAD = 128
Q_LORA = D_MODEL // 2
KV_LORA = D_MODEL // 4
ROPE_THETA = 10000.0
ATTN_SCALE = (QK_NOPE + QK_ROPE) ** -0.5
D_FF = 11 * D_MODEL // 4
CONV_W = 3
EPS = 1e-6
PAGE_SIZE = 128

LANES = 128
BF16_ROWS = 16
VMEM_LIMIT = 56 * 1024 * 1024

KCAT = KV_LORA + LANES
FF_TILE = 256
N_FF = D_FF // FF_TILE
TM_FFN = 512
TM_SGU = 512
TQ = 256
DEC_CHUNK_PAGES = 8


def _rms(x, g):
    return x * lax.rsqrt(jnp.mean(x * x, axis=-1, keepdims=True) + EPS) * g


def _gelu_erf(x):
    return 0.5 * x * (1.0 + lax.erf(x * (0.5 ** 0.5)))


def _dot(a, b):
    return jnp.dot(a, b, preferred_element_type=F32)


def _dot_nt(a, b):
    return lax.dot_general(a, b, (((1,), (1,)), ((), ())), preferred_element_type=F32)


def _params(n_grid_axes):
    return pltpu.CompilerParams(
        dimension_semantics=("arbitrary",) * n_grid_axes, vmem_limit_bytes=VMEM_LIMIT)


def _resident(shape):
    nd = len(shape)
    return pl.BlockSpec(shape, lambda *_: (0,) * nd, pipeline_mode=pl.Buffered(1))


def _layer_resident(stacked, layer):
    nd = stacked.ndim - 1
    return pl.BlockSpec((None,) + stacked.shape[1:], lambda *_: (layer,) + (0,) * nd,
                        pipeline_mode=pl.Buffered(1))


def _ffn_prompt_kernel(x_ref, halo_ref, pre_g_ref, post_g_ref, wup_ref, cw_ref, cb_ref, wdn_ref,
                       y_ref, tail_ref, act_ref, h_ref):
    tm = x_ref.shape[1]
    ng = tm // 8
    i = pl.program_id(1)

    @pl.when(i >= 0)
    def _():
        g = pre_g_ref[...]
        hn = _rms(x_ref[0], g)
        h_ref[:tm] = jnp.swapaxes(hn.reshape(8, ng, hn.shape[-1]), 0, 1).reshape(tm, -1).astype(BF16)
        h_ref[tm:] = jnp.where(i > 0, _rms(halo_ref[0], g), 0.0).astype(BF16)

    first = lax.broadcasted_iota(jnp.int32, (8, FF_TILE), 0) == 0

    def conv(col0):
        sl = slice(col0, col0 + FF_TILE)
        a = _dot(h_ref[...], wup_ref[:, sl])
        ap = a[:tm]
        hist = a[tm + 8:]
        sh1 = jnp.where(first, pltpu.roll(hist, 1, 0), pltpu.roll(ap[tm - 8:], 1, 0))
        sh2 = jnp.where(first, pltpu.roll(hist, 2, 0), pltpu.roll(ap[tm - 16:tm - 8], 1, 0))
        w = cw_ref[:, sl]
        c = cb_ref[:, sl] + jnp.concatenate([sh2, sh1, ap[:tm - 16]], axis=0) * w[0:1]
        c = c + jnp.concatenate([sh1, ap[:tm - 8]], axis=0) * w[1:2]
        c = c + ap * w[2:3]
        tail_ref[0, 0, :, sl] = ap[tm - 16:]
        return c

    for f in range(N_FF):
        cg = conv(f * FF_TILE)
        cu = conv(D_FF + f * FF_TILE)
        act_ref[:, f * FF_TILE:(f + 1) * FF_TILE] = (jax.nn.gelu(cg, approximate=True) * cu).astype(BF16)
    fo = _rms(_dot(act_ref[...], wdn_ref[...]), post_g_ref[...])
    y_ref[0] = x_ref[0] + jnp.swapaxes(fo.reshape(ng, 8, fo.shape[-1]), 0, 1).reshape(tm, -1)


def _ffn_prompt(x, pre_g, post_g, layer, wup, cw, cb, wdn):
    b, t, d = x.shape
    tm = TM_FFN
    nt = t // tm
    halo_blocks = tm // BF16_ROWS
    y, tail = pl.pallas_call(
        _ffn_prompt_kernel,
        grid=(b, nt),
        in_specs=[
            pl.BlockSpec((1, tm, d), lambda bi, i: (bi, i, 0)),
            pl.BlockSpec((1, BF16_ROWS, d), lambda bi, i: (bi, jnp.maximum(i * halo_blocks - 1, 0), 0)),
            _resident((1, d)), _resident((1, d)),
            _layer_resident(wup, layer), _resident(cw.shape), _resident(cb.shape),
            _layer_resident(wdn, layer),
        ],
        out_specs=[
            pl.BlockSpec((1, tm, d), lambda bi, i: (bi, i, 0)),
            pl.BlockSpec((1, 1, 16, 2 * D_FF), lambda bi, i: (bi, i, 0, 0)),
        ],
        out_shape=[
            jax.ShapeDtypeStruct((b, t, d), F32),
            jax.ShapeDtypeStruct((b, nt, 16, 2 * D_FF), F32),
        ],
        scratch_shapes=[pltpu.VMEM((tm, D_FF), BF16), pltpu.VMEM((tm + BF16_ROWS, d), BF16)],
        compiler_params=_params(2),
        name="ffn_prompt",
    )(x, x, pre_g, post_g, wup, cw, cb, wdn)
    return y, tail[:, nt - 1, 7::8, :]


def _ffn_sample_kernel(x_ref, p0g_ref, p0u_ref, p1g_ref, p1u_ref, pre_g_ref, post_g_ref,
                       wg_ref, wu_ref, cwg_ref, cwu_ref, cbg_ref, cbu_ref, wdn_ref,
                       y_ref, ag_ref, au_ref, h_ref, acc_ref):
    f = pl.program_id(0)

    @pl.when(f == 0)
    def _():
        h_ref[...] = _rms(x_ref[...], pre_g_ref[...]).astype(BF16)

    h = h_ref[...]

    def conv(w_ref, p0_ref, p1_ref, cw_ref, cb_ref, a_ref):
        a = _dot(h, w_ref[...])
        a_ref[...] = a
        w = cw_ref[...]
        c = cb_ref[...] + p0_ref[...] * w[0:1]
        c = c + p1_ref[...] * w[1:2]
        return c + a * w[2:3]

    cg = conv(wg_ref, p0g_ref, p1g_ref, cwg_ref, cbg_ref, ag_ref)
    cu = conv(wu_ref, p0u_ref, p1u_ref, cwu_ref, cbu_ref, au_ref)
    act = (jax.nn.gelu(cg, approximate=True) * cu).astype(BF16)
    part = _dot(act, wdn_ref[...])

    @pl.when(f == 0)
    def _():
        acc_ref[...] = part

    @pl.when(f > 0)
    def _():
        acc_ref[...] += part

    @pl.when(f == pl.num_programs(0) - 1)
    def _():
        y_ref[...] = x_ref[...] + _rms(acc_ref[...], post_g_ref[...])


def _ffn_sample(x, state, pre_g, post_g, layer, wup, cw, cb, wdn):
    n, d = x.shape
    tf = FF_TILE
    st = state.reshape(n, (CONV_W - 1) * 2 * D_FF)
    hist = 2 * D_FF // tf
    col = lambda off: pl.BlockSpec((n, tf), lambda f: (0, off + f))
    row3 = lambda off: pl.BlockSpec((CONV_W, tf), lambda f: (0, off + f))
    row1 = lambda off: pl.BlockSpec((1, tf), lambda f: (0, off + f))
    const = lambda shape: pl.BlockSpec(shape, lambda f: (0,) * len(shape))
    y, ag, au = pl.pallas_call(
        _ffn_sample_kernel,
        grid=(N_FF,),
        in_specs=[
            const((n, d)),
            col(0), col(N_FF), col(hist), col(hist + N_FF),
            const((1, d)), const((1, d)),
            pl.BlockSpec((None, d, tf), lambda f: (layer, 0, f)),
            pl.BlockSpec((None, d, tf), lambda f: (layer, 0, N_FF + f)),
            row3(0), row3(N_FF), row1(0), row1(N_FF),
            pl.BlockSpec((None, tf, d), lambda f: (layer, f, 0)),
        ],
        out_specs=[const((n, d)), col(0), col(0)],
        out_shape=[
            jax.ShapeDtypeStruct((n, d), F32),
            jax.ShapeDtypeStruct((n, D_FF), F32),
            jax.ShapeDtypeStruct((n, D_FF), F32),
        ],
        scratch_shapes=[pltpu.VMEM((n, d), BF16), pltpu.VMEM((n, d), F32)],
        compiler_params=_params(1),
        name="ffn_sample",
    )(x, st, st, st, st, pre_g, post_g, wup, wup, cw, cw, cb, cb, wdn)
    new_state = jnp.stack([state[:, 1, :], jnp.concatenate([ag, au], axis=-1)], axis=1)
    return y, new_state


def _sgu_prompt_kernel(x_ref, pre_g_ref, post_g_ref, win_ref, sgug_ref, ws_ref, bs_ref, wout_ref,
                       y_ref, v_ref, acc_ref):
    tm = x_ref.shape[1]
    gd = SGU_GROUP_DIM
    x = x_ref[0]
    h = _rms(x, pre_g_ref[...]).astype(BF16)
    ss = jnp.zeros((tm, 1), F32)
    for j in range(SGU_GROUPS):
        zc = _gelu_erf(_dot(h, win_ref[:, SGU_WIDTH + j * gd:SGU_WIDTH + (j + 1) * gd]))
        v_ref[:, j * gd:(j + 1) * gd] = zc
        ss = ss + jnp.sum(zc * zc, axis=-1, keepdims=True)
    inv = lax.rsqrt(ss * (1.0 / SGU_WIDTH) + EPS)
    causal = (lax.broadcasted_iota(jnp.int32, (CHUNK, CHUNK), 0)
              >= lax.broadcasted_iota(jnp.int32, (CHUNK, CHUNK), 1))
    for g in range(SGU_GROUPS):
        sl = slice(g * gd, (g + 1) * gd)
        vn = (v_ref[:, sl] * inv * sgug_ref[:, sl]).astype(BF16)
        wm = jnp.where(causal, ws_ref[g], 0.0).astype(BF16)
        bias = bs_ref[g]
        s = jnp.concatenate(
            [_dot(wm, vn[c * CHUNK:(c + 1) * CHUNK]) + bias for c in range(tm // CHUNK)], axis=0)
        u = _gelu_erf(_dot(h, win_ref[:, sl]))
        part = _dot((u * s).astype(BF16), wout_ref[sl, :])
        if g == 0:
            acc_ref[...] = part
        else:
            acc_ref[...] += part
    y_ref[0] = x + _rms(acc_ref[...], post_g_ref[...])


def _sgu_prompt(x, pre_g, post_g, layer, win, sgug, ws, bs, wout):
    b, t, d = x.shape
    tm = TM_SGU
    return pl.pallas_call(
        _sgu_prompt_kernel,
        grid=(b, t // tm),
        in_specs=[
            pl.BlockSpec((1, tm, d), lambda bi, i: (bi, i, 0)),
            _resident((1, d)), _resident((1, d)),
            _layer_resident(win, layer), _resident(sgug.shape), _resident(ws.shape), _resident(bs.shape),
            _layer_resident(wout, layer),
        ],
        out_specs=pl.BlockSpec((1, tm, d), lambda bi, i: (bi, i, 0)),
        out_shape=jax.ShapeDtypeStruct((b, t, d), F32),
        scratch_shapes=[pltpu.VMEM((tm, SGU_WIDTH), F32), pltpu.VMEM((tm, d), F32)],
        compiler_params=_params(2),
        name="sgu_prompt",
    )(x, pre_g, post_g, win, sgug, ws, bs, wout)


def _sgu_sample_kernel(x_ref, pre_g_ref, post_g_ref, win_ref, sgug_ref, wdiag_ref, bdiag_ref, wout_ref,
                       y_ref, v_ref):
    x = x_ref[...]
    h = _rms(x, pre_g_ref[...]).astype(BF16)
    u = _gelu_erf(_dot(h, win_ref[:, :SGU_WIDTH]))
    v = _gelu_erf(_dot(h, win_ref[:, SGU_WIDTH:]))
    vn = _rms(v, sgug_ref[...])
    v_ref[...] = vn
    s = vn * wdiag_ref[...] + bdiag_ref[...]
    y = _dot((u * s).astype(BF16), wout_ref[...])
    y_ref[...] = x + _rms(y, post_g_ref[...])


def _sgu_sample(x, pre_g, post_g, layer, win, sgug, wdiag, bdiag, wout):
    n, d = x.shape
    whole = lambda shape: pl.BlockSpec(shape, lambda i: (0,) * len(shape))
    return pl.pallas_call(
        _sgu_sample_kernel,
        grid=(1,),
        in_specs=[whole((n, d)), whole((1, d)), whole((1, d)), _layer_resident(win, layer),
                  whole((1, SGU_WIDTH)), whole((1, SGU_WIDTH)), whole((1, SGU_WIDTH)),
                  _layer_resident(wout, layer)],
        out_specs=[whole((n, d)), whole((n, SGU_WIDTH))],
        out_shape=[jax.ShapeDtypeStruct((n, d), F32), jax.ShapeDtypeStruct((n, SGU_WIDTH), F32)],
        compiler_params=_params(1),
        name="sgu_sample",
    )(x, pre_g, post_g, win, sgug, wdiag, bdiag, wout)


def _kv_side_kernel(x_ref, cs_ref, g_in_ref, wc_ref, wpe_ref, kvg_ref, ckv_ref, kr_ref, kcat_ref, ckvt_ref):
    h = _rms(x_ref[0], g_in_ref[...]).astype(BF16)
    ckv = _rms(_dot(h, wc_ref[...]), kvg_ref[...])
    t = _dot(h, wpe_ref[...]) * cs_ref[...]
    t = t + pltpu.roll(t, QK_ROPE, 1)
    lane = lax.broadcasted_iota(jnp.int32, t.shape, 1)
    krp = jnp.where(lane < QK_ROPE, t, 0.0)
    ckv_ref[0] = ckv
    kr_ref[0] = t[:, :QK_ROPE]
    kcat_ref[0] = jnp.concatenate([ckv, krp], axis=1).astype(BF16)
    ckvt_ref[0, 0] = ckv.T.astype(BF16)


def _kv_side(x, cs_tab, g_in, wc, wpe, kvg, tm):
    b, t, d = x.shape
    return pl.pallas_call(
        _kv_side_kernel,
        grid=(b, t // tm),
        in_specs=[
            pl.BlockSpec((1, tm, d), lambda bi, i: (bi, i, 0)),
            pl.BlockSpec((tm, LANES), lambda bi, i: (i, 0)),
            _resident((1, d)), _resident(wc.shape), _resident(wpe.shape), _resident((1, KV_LORA)),
        ],
        out_specs=[
            pl.BlockSpec((1, tm, KV_LORA), lambda bi, i: (bi, i, 0)),
            pl.BlockSpec((1, tm, QK_ROPE), lambda bi, i: (bi, i, 0)),
            pl.BlockSpec((1, tm, KCAT), lambda bi, i: (bi, i, 0)),
            pl.BlockSpec((1, 1, KV_LORA, tm), lambda bi, i: (bi, i, 0, 0)),
        ],
        out_shape=[
            jax.ShapeDtypeStruct((b, t, KV_LORA), F32),
            jax.ShapeDtypeStruct((b, t, QK_ROPE), F32),
            jax.ShapeDtypeStruct((b, t, KCAT), BF16),
            jax.ShapeDtypeStruct((b, t // tm, KV_LORA, tm), BF16),
        ],
        compiler_params=_params(2),
        name="kv_side",
    )(x, cs_tab, g_in, wc, wpe, kvg)


def _q_rows(x, cos, sin, pre_g, wdq, qg, wuq_n, wuq_pa, wuq_pb, wuk_t, store):
    h = _rms(x, pre_g).astype(BF16)
    cq = _rms(_dot(h, wdq[...]), qg).astype(BF16)
    qn = _dot(cq, wuq_n[...])
    qa = _dot(cq, wuq_pa[...])
    qb = _dot(cq, wuq_pb[...])
    for hd in range(N_HEADS):
        sl = slice(hd * LANES, (hd + 1) * LANES)
        qpe = qa[:, sl] * cos + qb[:, sl] * sin
        ql = _dot(qn[:, sl].astype(BF16), wuk_t[hd])
        store(hd, (ql * ATTN_SCALE).astype(BF16), (qpe * ATTN_SCALE).astype(BF16))


def _mla_out(o_heads, x, post_g, wuv, wo, o_ref):
    for hd in range(N_HEADS):
        o_ref[:, hd * V_HEAD:(hd + 1) * V_HEAD] = _dot(o_heads(hd).astype(BF16), wuv[hd]).astype(BF16)
    m = _dot(o_ref[...], wo[...])
    return x + _rms(m, post_g)


def _mla_prompt_kernel(x_ref, k_ref, vt_ref, cos_ref, sin_ref, pre_g_ref, post_g_ref, wdq_ref, qg_ref,
                       wuqn_ref, wuqa_ref, wuqb_ref, wuk_ref, wuvt_ref, wot_ref,
                       y_ref, qt_ref, m_ref, l_ref, acc_ref, o_ref):
    tq = x_ref.shape[1]
    qi = pl.program_id(1)
    x = x_ref[0]

    h = _rms(x, pre_g_ref[...]).astype(BF16)
    cq = _rms(_dot(h, wdq_ref[...]), qg_ref[...])
    cq_t = cq.T.astype(BF16)
    qn_t = _dot(wuqn_ref[...], cq_t)
    qa_t = _dot(wuqa_ref[...], cq_t)
    qb_t = _dot(wuqb_ref[...], cq_t)
    cos_t, sin_t = cos_ref[...], sin_ref[...]
    for hd in range(N_HEADS):
        cols = slice(hd * tq, (hd + 1) * tq)
        ql_t = _dot(wuk_ref[hd], qn_t[hd * QK_NOPE:(hd + 1) * QK_NOPE].astype(BF16))
        rows = slice(hd * QK_ROPE, (hd + 1) * QK_ROPE)
        qpe_t = qa_t[rows] * cos_t + qb_t[rows] * sin_t
        qt_ref[:KV_LORA, cols] = (ql_t * ATTN_SCALE).astype(BF16)
        qt_ref[KV_LORA:KV_LORA + QK_ROPE, cols] = (qpe_t * ATTN_SCALE).astype(BF16)
    qt_ref[KV_LORA + QK_ROPE:, :] = jnp.zeros((KCAT - KV_LORA - QK_ROPE, N_HEADS * tq), BF16)

    m_ref[...] = jnp.full(m_ref.shape, -jnp.inf, F32)
    l_ref[...] = jnp.zeros(l_ref.shape, F32)
    acc_ref[...] = jnp.zeros(acc_ref.shape, F32)

    def step(kb, diagonal):
        k = k_ref[0, pl.ds(pl.multiple_of(kb * tq, tq), tq), :]
        vt = vt_ref[0, kb]
        scores = [_dot(k, qt_ref[:, hd * tq:(hd + 1) * tq]) for hd in range(N_HEADS)]
        for hd in range(N_HEADS):
            cols = slice(hd * tq, (hd + 1) * tq)
            st = scores[hd]
            if diagonal:
                key = lax.broadcasted_iota(jnp.int32, st.shape, 0)
                qry = lax.broadcasted_iota(jnp.int32, st.shape, 1)
                st = jnp.where(key <= qry, st, -jnp.inf)
            m_old = m_ref[hd:hd + 1, :]
            m_new = jnp.maximum(m_old, jnp.max(st, axis=0, keepdims=True))
            alpha = jnp.exp(m_old - m_new)
            p = jnp.exp(st - m_new)
            l_ref[hd:hd + 1, :] = l_ref[hd:hd + 1, :] * alpha + jnp.sum(p, axis=0, keepdims=True)
            acc_ref[:, cols] = acc_ref[:, cols] * alpha + _dot(vt, p.astype(BF16))
            m_ref[hd:hd + 1, :] = m_new

    def body(kb, carry):
        step(kb, False)
        return carry

    lax.fori_loop(0, qi, body, 0)
    step(qi, True)

    for hd in range(N_HEADS):
        cols = slice(hd * tq, (hd + 1) * tq)
        o_t = acc_ref[:, cols] * (1.0 / l_ref[hd:hd + 1, :])
        o_ref[hd * V_HEAD:(hd + 1) * V_HEAD, :] = _dot(wuvt_ref[hd], o_t.astype(BF16)).astype(BF16)
    m_t = _dot(wot_ref[...], o_ref[...])
    y_ref[0] = x + _rms(m_t.T, post_g_ref[...])


def _mla_prompt(x, kcat, ckv_t, cos_t, sin_t, pre_g, post_g, wdq, qg, wuqn_t, wuqa_t, wuqb_t, wuk_h,
                wuvt_h, wo_t):
    b, t, d = x.shape
    tq = TQ
    lanes = N_HEADS * tq
    weights = (wdq, qg, wuqn_t, wuqa_t, wuqb_t, wuk_h, wuvt_h, wo_t)
    return pl.pallas_call(
        _mla_prompt_kernel,
        grid=(b, t // tq),
        in_specs=[
            pl.BlockSpec((1, tq, d), lambda bi, i: (bi, i, 0)),
            pl.BlockSpec((1, t, KCAT), lambda bi, i: (bi, 0, 0)),
            pl.BlockSpec((1, t // tq, KV_LORA, tq), lambda bi, i: (bi, 0, 0, 0)),
            pl.BlockSpec((QK_ROPE, tq), lambda bi, i: (0, i)),
            pl.BlockSpec((QK_ROPE, tq), lambda bi, i: (0, i)),
            _resident((1, d)), _resident((1, d)),
        ] + [_resident(w.shape) for w in weights],
        out_specs=pl.BlockSpec((1, tq, d), lambda bi, i: (bi, i, 0)),
        out_shape=jax.ShapeDtypeStruct((b, t, d), F32),
        scratch_shapes=[
            pltpu.VMEM((KCAT, lanes), BF16),
            pltpu.VMEM((N_HEADS, tq), F32), pltpu.VMEM((N_HEADS, tq), F32),
            pltpu.VMEM((KV_LORA, lanes), F32),
            pltpu.VMEM((N_HEADS * V_HEAD, tq), BF16),
        ],
        compiler_params=_params(2),
        name="mla_prompt",
    )(x, kcat, ckv_t, cos_t, sin_t, pre_g, post_g, *weights)


def _q_sample_kernel(x_ref, cos_ref, sin_ref, pre_g_ref, wdq_ref, qg_ref, wuqn_ref, wuqa_ref, wuqb_ref,
                     wukt_ref, q_ref):
    def store(hd, ql, qpe):
        q_ref[hd, :, :KV_LORA] = ql
        q_ref[hd, :, KV_LORA:] = qpe

    _q_rows(x_ref[...], cos_ref[...], sin_ref[...], pre_g_ref[...], wdq_ref, qg_ref[...],
            wuqn_ref, wuqa_ref, wuqb_ref, wukt_ref, store)


def _q_sample(x, cos, sin, pre_g, wdq, qg, wuq_n, wuq_pa, wuq_pb, wuk_t):
    n = x.shape[0]
    return pl.pallas_call(
        _q_sample_kernel,
        out_shape=jax.ShapeDtypeStruct((N_HEADS, n, KCAT), BF16),
        compiler_params=pltpu.CompilerParams(vmem_limit_bytes=VMEM_LIMIT),
        name="q_sample",
    )(x, cos, sin, pre_g, wdq, qg, wuq_n, wuq_pa, wuq_pb, wuk_t)


def _decode_kernel(pt_ref, q_ref, knew_ref, ckv_hbm, krt_hbm, o_ref, cbuf, rbuf, sem):
    b = pl.program_id(0)
    nb = pl.num_programs(0)
    n_pages = cbuf.shape[1]
    n_chunks = n_pages // DEC_CHUNK_PAGES
    cp_rows = DEC_CHUNK_PAGES * PAGE_SIZE

    def page_copies(page, slot, p):
        return (pltpu.make_async_copy(ckv_hbm.at[page], cbuf.at[slot, p], sem.at[0, slot]),
                pltpu.make_async_copy(krt_hbm.at[page], rbuf.at[slot, p], sem.at[1, slot]))

    def start_pages(seq, slot, p_lo, p_hi):
        for p in range(p_lo, p_hi):
            for cp in page_copies(pt_ref[seq, p], slot, p):
                cp.start()

    def wait_slot(slot):
        for p in range(n_pages):
            for cp in page_copies(0, slot, p):
                cp.wait()

    slot = b % 2
    nxt = jnp.minimum(b + 1, nb - 1)

    @pl.when(b == 0)
    def _():
        start_pages(0, 0, 0, n_pages)

    wait_slot(slot)

    q = q_ref[0]
    ql = q[:, :KV_LORA]
    qp = q[:, KV_LORA:KV_LORA + QK_ROPE]

    def chunk_scores(c):
        start_pages(nxt, 1 - slot, c * DEC_CHUNK_PAGES, (c + 1) * DEC_CHUNK_PAGES)
        s_pe = jnp.concatenate(
            [_dot(qp, rbuf[slot, c * DEC_CHUNK_PAGES + j].astype(BF16)) for j in range(DEC_CHUNK_PAGES)],
            axis=1)
        rows = cbuf[slot, c * DEC_CHUNK_PAGES:(c + 1) * DEC_CHUNK_PAGES].reshape(cp_rows, KV_LORA)
        return _dot(ql, rows.T.astype(BF16)) + s_pe, rows.astype(BF16)

    kn = knew_ref[0].astype(F32)
    m = jnp.sum(q.astype(F32) * kn, axis=-1, keepdims=True)
    l = jnp.ones_like(m)
    acc = jnp.broadcast_to(kn[:, :KV_LORA], (N_HEADS, KV_LORA))
    nxt_chunk = chunk_scores(0)
    for c in range(n_chunks):
        s, kc = nxt_chunk
        if c + 1 < n_chunks:
            nxt_chunk = chunk_scores(c + 1)
        m_new = jnp.maximum(m, jnp.max(s, axis=-1, keepdims=True))
        alpha = jnp.exp(m - m_new)
        p = jnp.exp(s - m_new)
        l = l * alpha + jnp.sum(p, axis=-1, keepdims=True)
        acc = acc * alpha + _dot(p.astype(BF16), kc)
        m = m_new
    o_ref[0] = acc / l

    @pl.when(b == nb - 1)
    def _():
        wait_slot(1 - slot)


def _decode(page_table, q, knew, cache_ckv, cache_krt):
    n, n_pages = page_table.shape
    grid_spec = pltpu.PrefetchScalarGridSpec(
        num_scalar_prefetch=1,
        grid=(n,),
        in_specs=[
            pl.BlockSpec((1, N_HEADS, KCAT), lambda bi, pt: (bi, 0, 0)),
            pl.BlockSpec((1, 1, KCAT), lambda bi, pt: (bi, 0, 0)),
            pl.BlockSpec(memory_space=pl.ANY),
            pl.BlockSpec(memory_space=pl.ANY),
        ],
        out_specs=pl.BlockSpec((1, N_HEADS, KV_LORA), lambda bi, pt: (bi, 0, 0)),
        scratch_shapes=[
            pltpu.VMEM((2, n_pages, PAGE_SIZE, KV_LORA), F32),
            pltpu.VMEM((2, n_pages, QK_ROPE, PAGE_SIZE), F32),
            pltpu.SemaphoreType.DMA((2, 2)),
        ],
    )
    return pl.pallas_call(
        _decode_kernel,
        grid_spec=grid_spec,
        out_shape=jax.ShapeDtypeStruct((n, N_HEADS, KV_LORA), F32),
        compiler_params=_params(1),
        name="mla_decode",
    )(page_table, q, knew, cache_ckv, cache_krt)


def _out_sample_kernel(o_ref, x_ref, post_g_ref, wuv_ref, wo_ref, y_ref, os_ref):
    y_ref[...] = _mla_out(lambda hd: o_ref[hd], x_ref[...], post_g_ref[...], wuv_ref, wo_ref, os_ref)


def _out_sample(o_heads, x, post_g, wuv, wo):
    n, d = x.shape
    return pl.pallas_call(
        _out_sample_kernel,
        out_shape=jax.ShapeDtypeStruct((n, d), F32),
        scratch_shapes=[pltpu.VMEM((n, N_HEADS * V_HEAD), BF16)],
        compiler_params=pltpu.CompilerParams(vmem_limit_bytes=VMEM_LIMIT),
        name="out_sample",
    )(o_heads, x, post_g, wuv, wo)


def _rot_half_cols(w):
    half = QK_ROPE // 2
    return jnp.concatenate([-w[..., half:], w[..., :half]], axis=-1)


def _rope_tables(pos):
    half = QK_ROPE // 2
    inv = 1.0 / (ROPE_THETA ** (jnp.arange(half, dtype=F32) / half))
    ang = pos.astype(F32)[:, None] * inv[None, :]
    cos, sin = jnp.cos(ang), jnp.sin(ang)
    cos64 = jnp.concatenate([cos, cos], axis=-1)
    sin64 = jnp.concatenate([sin, sin], axis=-1)
    return cos64, sin64


def kernel(x_prompt, x_sample, cache_ckv, cache_kr, state_conv, page_table, pre_mix_g, post_mix_g,
           pre_ffn_g, post_ffn_g, w_in_a, sgu_g, w_s, b_s, w_out_a, kv_in_g, w_dkv, kv_g, w_uk, w_uv,
           w_dq, q_g, w_uq, w_o, w_up, conv_w, conv_b, w_down):
    depth = w_up.shape[0]
    n_a = w_in_a.shape[0]
    bp, tp, _ = x_prompt.shape
    ns = x_sample.shape[0]
    past_len = page_table.shape[1] * PAGE_SIZE
    row = lambda v: v.reshape(1, -1)

    w_in_b, w_out_b = w_in_a.astype(BF16), w_out_a.astype(BF16)
    w_up_b, w_down_b = w_up.astype(BF16), w_down.astype(BF16)
    w_dq_b, w_o_b = w_dq.astype(BF16), w_o.astype(BF16)
    wuq = w_uq.reshape(-1, Q_LORA, N_HEADS, QK_NOPE + QK_ROPE)
    wuq_n = wuq[..., :QK_NOPE].reshape(-1, Q_LORA, N_HEADS * QK_NOPE).astype(BF16)
    pad = lambda w: jnp.pad(w, ((0, 0),) * 3 + ((0, LANES - QK_ROPE),)).reshape(
        -1, Q_LORA, N_HEADS * LANES).astype(BF16)
    wuq_pa = pad(wuq[..., QK_NOPE:])
    wuq_pb = pad(_rot_half_cols(wuq[..., QK_NOPE:]))
    wuk_t = jnp.transpose(w_uk, (1, 2, 0)).astype(BF16)
    wuv_h = jnp.transpose(w_uv, (1, 0, 2)).astype(BF16)
    wuqn_t = jnp.swapaxes(wuq_n, 1, 2)
    flat_t = lambda w: jnp.swapaxes(w.reshape(-1, Q_LORA, N_HEADS * QK_ROPE), 1, 2).astype(BF16)
    wuqa_t = flat_t(wuq[..., QK_NOPE:])
    wuqb_t = flat_t(_rot_half_cols(wuq[..., QK_NOPE:]))
    wuk_h = jnp.transpose(w_uk, (1, 0, 2)).astype(BF16)
    wuvt_h = jnp.transpose(w_uv, (1, 2, 0)).astype(BF16)
    wo_t = jnp.swapaxes(w_o_b, 1, 2)
    cache_krt = jnp.swapaxes(cache_kr, 1, 2)
    w_c = w_dkv[:, :KV_LORA].astype(BF16)
    w_pe = w_dkv[:, KV_LORA:]
    w_pe2 = jnp.concatenate([w_pe, _rot_half_cols(w_pe)], axis=-1).astype(BF16)
    bs_col = b_s[..., None]
    wdiag = jnp.repeat(w_s[:, :, 0, 0], SGU_GROUP_DIM, axis=-1)
    bdiag = jnp.repeat(b_s[:, :, 0], SGU_GROUP_DIM, axis=-1)

    cos_p, sin_p = _rope_tables(jnp.arange(tp, dtype=jnp.int32))
    cos_s, sin_s = _rope_tables(jnp.full((ns,), past_len, dtype=jnp.int32))
    tile2 = lambda a: jnp.concatenate([a, a], axis=-1)
    cs_p = jnp.concatenate([cos_p, sin_p], axis=-1)
    cs_s = jnp.concatenate([cos_s, sin_s], axis=-1)

    xp = x_prompt
    xs = x_sample.reshape(ns, D_MODEL)
    conv_p, conv_s, v_rows = [], [], []
    kcat_p = kcat_s = ckvt_p = ckv_p = kr_p = ckv_s = kr_s = None
    for layer in range(depth):
        if layer == n_a:
            ckv_p, kr_p, kcat_p, ckvt_p = _kv_side(xp, cs_p, row(kv_in_g), w_c, w_pe2, row(kv_g), TQ)
            ckv_s, kr_s, kcat_s, _ = _kv_side(xs[None], cs_s, row(kv_in_g), w_c, w_pe2, row(kv_g), ns)
        pre_g, post_g = row(pre_mix_g[layer]), row(post_mix_g[layer])
        if layer < n_a:
            xp = _sgu_prompt(xp, pre_g, post_g, layer, w_in_b, row(sgu_g[layer]), w_s[layer],
                             bs_col[layer], w_out_b)
            xs, v = _sgu_sample(xs, pre_g, post_g, layer, w_in_b, row(sgu_g[layer]), row(wdiag[layer]),
                                row(bdiag[layer]), w_out_b)
            v_rows.append(v)
        else:
            j = layer - n_a
            qw = (w_dq_b[j], row(q_g[j]), wuq_n[j], wuq_pa[j], wuq_pb[j], wuk_t)
            xp = _mla_prompt(xp, kcat_p, ckvt_p, cos_p.T, sin_p.T, pre_g, post_g, w_dq_b[j], row(q_g[j]),
                             wuqn_t[j], wuqa_t[j], wuqb_t[j], wuk_h, wuvt_h, wo_t[j])
            q = _q_sample(xs, tile2(cos_s), tile2(sin_s), pre_g, *qw)
            o = _decode(page_table, jnp.transpose(q, (1, 0, 2)), kcat_s.reshape(ns, 1, KCAT),
                        cache_ckv, cache_krt)
            xs = _out_sample(jnp.transpose(o, (1, 0, 2)), xs, post_g, wuv_h, w_o_b[j])
        pre_g, post_g = row(pre_ffn_g[layer]), row(post_ffn_g[layer])
        xp, tail = _ffn_prompt(xp, pre_g, post_g, layer, w_up_b, conv_w[layer], row(conv_b[layer]),
                               w_down_b)
        conv_p.append(tail)
        xs, st = _ffn_sample(xs, state_conv[layer], pre_g, post_g, layer, w_up_b, conv_w[layer],
                             row(conv_b[layer]), w_down_b)
        conv_s.append(st)

    return (xp, xs.reshape(ns, 1, D_MODEL), ckv_p, kr_p, jnp.stack(conv_p),
            ckv_s.reshape(ns, 1, KV_LORA), kr_s.reshape(ns, 1, QK_ROPE), jnp.stack(conv_s),
            jnp.stack(v_rows).reshape(n_a, ns, 1, SGU_WIDTH))
```

```python
import jax
import jax.numpy as jnp
from jax import lax
from jax.experimental import pallas as pl
from jax.experimental.pallas import tpu as pltpu

F32 = jnp.float32
BF16 = jnp.bfloat16

D_MODEL = 1024
N_A_LAYERS = 2
CHUNK = 128
SGU_WIDTH = 2 * D_MODEL
SGU_GROUPS = 8
SGU_GROUP_DIM = SGU_WIDTH // SGU_GROUPS
N_HEADS = 8
QK_NOPE = 128
QK_ROPE = 64
V_HEAD = 128
Q_LORA = D_MODEL // 2
KV_LORA = D_MODEL // 4
ROPE_THETA = 10000.0
ATTN_SCALE = (QK_NOPE + QK_ROPE) ** -0.5
D_FF = 11 * D_MODEL // 4
CONV_W = 3
EPS = 1e-6
PAGE_SIZE = 128

LANES = 128
BF16_ROWS = 16
VMEM_LIMIT = 56 * 1024 * 1024

KCAT = KV_LORA + LANES
FF_TILE = 256
N_FF = D_FF // FF_TILE
TM_FFN = 512
TM_SGU = 512
TQ = 256
DEC_CHUNK_PAGES = 8
DEC_SLOTS = 3


def _rms(x, g):
    return x * lax.rsqrt(jnp.mean(x * x, axis=-1, keepdims=True) + EPS) * g


def _gelu_erf(x):
    return 0.5 * x * (1.0 + lax.erf(x * (0.5 ** 0.5)))


def _dot(a, b):
    return jnp.dot(a, b, preferred_element_type=F32)


def _dot_nt(a, b):
    return lax.dot_general(a, b, (((1,), (1,)), ((), ())), preferred_element_type=F32)


def _params(n_grid_axes):
    return pltpu.CompilerParams(
        dimension_semantics=("arbitrary",) * n_grid_axes, vmem_limit_bytes=VMEM_LIMIT)


def _resident(shape):
    nd = len(shape)
    return pl.BlockSpec(shape, lambda *_: (0,) * nd, pipeline_mode=pl.Buffered(1))


def _layer_resident(stacked, layer):
    nd = stacked.ndim - 1
    return pl.BlockSpec((None,) + stacked.shape[1:], lambda *_: (layer,) + (0,) * nd,
                        pipeline_mode=pl.Buffered(1))


def _ffn_prompt_kernel(x_ref, halo_ref, pre_g_ref, post_g_ref, wup_ref, cw_ref, cb_ref, wdn_ref,
                       y_ref, tail_ref, act_ref, h_ref):
    tm = x_ref.shape[1]
    ng = tm // 8
    i = pl.program_id(1)

    @pl.when(i >= 0)
    def _():
        g = pre_g_ref[...]
        hn = _rms(x_ref[0], g)
        h_ref[:tm] = jnp.swapaxes(hn.reshape(8, ng, hn.shape[-1]), 0, 1).reshape(tm, -1).astype(BF16)
        h_ref[tm:] = jnp.where(i > 0, _rms(halo_ref[0], g), 0.0).astype(BF16)

    first = lax.broadcasted_iota(jnp.int32, (8, FF_TILE), 0) == 0

    def conv(col0):
        sl = slice(col0, col0 + FF_TILE)
        a = _dot(h_ref[...], wup_ref[:, sl])
        ap = a[:tm]
        hist = a[tm + 8:]
        sh1 = jnp.where(first, pltpu.roll(hist, 1, 0), pltpu.roll(ap[tm - 8:], 1, 0))
        sh2 = jnp.where(first, pltpu.roll(hist, 2, 0), pltpu.roll(ap[tm - 16:tm - 8], 1, 0))
        w = cw_ref[:, sl]
        c = cb_ref[:, sl] + jnp.concatenate([sh2, sh1, ap[:tm - 16]], axis=0) * w[0:1]
        c = c + jnp.concatenate([sh1, ap[:tm - 8]], axis=0) * w[1:2]
        c = c + ap * w[2:3]
        tail_ref[0, 0, :, sl] = ap[tm - 16:]
        return c

    for f in range(N_FF):
        cg = conv(f * FF_TILE)
        cu = conv(D_FF + f * FF_TILE)
        act_ref[:, f * FF_TILE:(f + 1) * FF_TILE] = (jax.nn.gelu(cg, approximate=True) * cu).astype(BF16)
    fo = _rms(_dot(act_ref[...], wdn_ref[...]), post_g_ref[...])
    y_ref[0] = x_ref[0] + jnp.swapaxes(fo.reshape(ng, 8, fo.shape[-1]), 0, 1).reshape(tm, -1)


def _ffn_prompt(x, pre_g, post_g, layer, wup, cw, cb, wdn):
    b, t, d = x.shape
    tm = TM_FFN
    nt = t // tm
    halo_blocks = tm // BF16_ROWS
    y, tail = pl.pallas_call(
        _ffn_prompt_kernel,
        grid=(b, nt),
        in_specs=[
            pl.BlockSpec((1, tm, d), lambda bi, i: (bi, i, 0)),
            pl.BlockSpec((1, BF16_ROWS, d), lambda bi, i: (bi, jnp.maximum(i * halo_blocks - 1, 0), 0)),
            _resident((1, d)), _resident((1, d)),
            _layer_resident(wup, layer), _resident(cw.shape), _resident(cb.shape),
            _layer_resident(wdn, layer),
        ],
        out_specs=[
            pl.BlockSpec((1, tm, d), lambda bi, i: (bi, i, 0)),
            pl.BlockSpec((1, 1, 16, 2 * D_FF), lambda bi, i: (bi, i, 0, 0)),
        ],
        out_shape=[
            jax.ShapeDtypeStruct((b, t, d), F32),
            jax.ShapeDtypeStruct((b, nt, 16, 2 * D_FF), F32),
        ],
        scratch_shapes=[pltpu.VMEM((tm, D_FF), BF16), pltpu.VMEM((tm + BF16_ROWS, d), BF16)],
        compiler_params=_params(2),
        name="ffn_prompt",
    )(x, x, pre_g, post_g, wup, cw, cb, wdn)
    return y, tail[:, nt - 1, 7::8, :]


def _ffn_sample_kernel(x_ref, p0g_ref, p0u_ref, p1g_ref, p1u_ref, pre_g_ref, post_g_ref,
                       wg_ref, wu_ref, cwg_ref, cwu_ref, cbg_ref, cbu_ref, wdn_ref,
                       y_ref, ag_ref, au_ref, h_ref, acc_ref):
    f = pl.program_id(0)

    @pl.when(f == 0)
    def _():
        h_ref[...] = _rms(x_ref[...], pre_g_ref[...]).astype(BF16)

    h = h_ref[...]

    def conv(w_ref, p0_ref, p1_ref, cw_ref, cb_ref, a_ref):
        a = _dot(h, w_ref[...])
        a_ref[...] = a
        w = cw_ref[...]
        c = cb_ref[...] + p0_ref[...] * w[0:1]
        c = c + p1_ref[...] * w[1:2]
        return c + a * w[2:3]

    cg = conv(wg_ref, p0g_ref, p1g_ref, cwg_ref, cbg_ref, ag_ref)
    cu = conv(wu_ref, p0u_ref, p1u_ref, cwu_ref, cbu_ref, au_ref)
    act = (jax.nn.gelu(cg, approximate=True) * cu).astype(BF16)
    part = _dot(act, wdn_ref[...])

    @pl.when(f == 0)
    def _():
        acc_ref[...] = part

    @pl.when(f > 0)
    def _():
        acc_ref[...] += part

    @pl.when(f == pl.num_programs(0) - 1)
    def _():
        y_ref[...] = x_ref[...] + _rms(acc_ref[...], post_g_ref[...])


def _ffn_sample(x, state, pre_g, post_g, layer, wup, cw, cb, wdn):
    n, d = x.shape
    tf = FF_TILE
    st = state.reshape(n, (CONV_W - 1) * 2 * D_FF)
    hist = 2 * D_FF // tf
    col = lambda off: pl.BlockSpec((n, tf), lambda f: (0, off + f))
    row3 = lambda off: pl.BlockSpec((CONV_W, tf), lambda f: (0, off + f))
    row1 = lambda off: pl.BlockSpec((1, tf), lambda f: (0, off + f))
    const = lambda shape: pl.BlockSpec(shape, lambda f: (0,) * len(shape))
    y, ag, au = pl.pallas_call(
        _ffn_sample_kernel,
        grid=(N_FF,),
        in_specs=[
            const((n, d)),
            col(0), col(N_FF), col(hist), col(hist + N_FF),
            const((1, d)), const((1, d)),
            pl.BlockSpec((None, d, tf), lambda f: (layer, 0, f)),
            pl.BlockSpec((None, d, tf), lambda f: (layer, 0, N_FF + f)),
            row3(0), row3(N_FF), row1(0), row1(N_FF),
            pl.BlockSpec((None, tf, d), lambda f: (layer, f, 0)),
        ],
        out_specs=[const((n, d)), col(0), col(0)],
        out_shape=[
            jax.ShapeDtypeStruct((n, d), F32),
            jax.ShapeDtypeStruct((n, D_FF), F32),
            jax.ShapeDtypeStruct((n, D_FF), F32),
        ],
        scratch_shapes=[pltpu.VMEM((n, d), BF16), pltpu.VMEM((n, d), F32)],
        compiler_params=_params(1),
        name="ffn_sample",
    )(x, st, st, st, st, pre_g, post_g, wup, wup, cw, cw, cb, cb, wdn)
    new_state = jnp.stack([state[:, 1, :], jnp.concatenate([ag, au], axis=-1)], axis=1)
    return y, new_state


def _sgu_prompt_kernel(x_ref, pre_g_ref, post_g_ref, win_ref, sgug_ref, ws_ref, bs_ref, wout_ref,
                       y_ref, v_ref, gated_ref, h_ref):
    tm = x_ref.shape[1]
    gd = SGU_GROUP_DIM
    x = x_ref[0]

    @pl.when(pl.program_id(1) >= 0)
    def _():
        h_ref[...] = _rms(x, pre_g_ref[...]).astype(BF16)

    ss = jnp.zeros((tm, 1), F32)
    for j in range(SGU_GROUPS):
        zc = _gelu_erf(_dot(h_ref[...], win_ref[:, SGU_WIDTH + j * gd:SGU_WIDTH + (j + 1) * gd]))
        v_ref[:, j * gd:(j + 1) * gd] = zc
        ss = ss + jnp.sum(zc * zc, axis=-1, keepdims=True)
    inv = lax.rsqrt(ss * (1.0 / SGU_WIDTH) + EPS)
    causal = (lax.broadcasted_iota(jnp.int32, (CHUNK, CHUNK), 0)
              >= lax.broadcasted_iota(jnp.int32, (CHUNK, CHUNK), 1))
    for g in range(SGU_GROUPS):
        sl = slice(g * gd, (g + 1) * gd)
        vn = (v_ref[:, sl] * inv * sgug_ref[:, sl]).astype(BF16)
        wm = jnp.where(causal, ws_ref[g], 0.0).astype(BF16)
        bias = bs_ref[g]
        s = jnp.concatenate(
            [_dot(wm, vn[c * CHUNK:(c + 1) * CHUNK]) + bias for c in range(tm // CHUNK)], axis=0)
        u = _gelu_erf(_dot(h_ref[...], win_ref[:, sl]))
        gated_ref[:, sl] = (u * s).astype(BF16)
    y_ref[0] = x + _rms(_dot(gated_ref[...], wout_ref[...]), post_g_ref[...])


def _sgu_prompt(x, pre_g, post_g, layer, win, sgug, ws, bs, wout):
    b, t, d = x.shape
    tm = TM_SGU
    return pl.pallas_call(
        _sgu_prompt_kernel,
        grid=(b, t // tm),
        in_specs=[
            pl.BlockSpec((1, tm, d), lambda bi, i: (bi, i, 0)),
            _resident((1, d)), _resident((1, d)),
            _layer_resident(win, layer), _resident(sgug.shape), _resident(ws.shape), _resident(bs.shape),
            _layer_resident(wout, layer),
        ],
        out_specs=pl.BlockSpec((1, tm, d), lambda bi, i: (bi, i, 0)),
        out_shape=jax.ShapeDtypeStruct((b, t, d), F32),
        scratch_shapes=[pltpu.VMEM((tm, SGU_WIDTH), F32), pltpu.VMEM((tm, SGU_WIDTH), BF16),
                        pltpu.VMEM((tm, d), BF16)],
        compiler_params=_params(2),
        name="sgu_prompt",
    )(x, pre_g, post_g, win, sgug, ws, bs, wout)


def _sgu_sample_kernel(x_ref, pre_g_ref, post_g_ref, win_ref, sgug_ref, wdiag_ref, bdiag_ref, wout_ref,
                       y_ref, v_ref):
    x = x_ref[...]
    h = _rms(x, pre_g_ref[...]).astype(BF16)
    u = _gelu_erf(_dot(h, win_ref[:, :SGU_WIDTH]))
    v = _gelu_erf(_dot(h, win_ref[:, SGU_WIDTH:]))
    vn = _rms(v, sgug_ref[...])
    v_ref[...] = vn
    s = vn * wdiag_ref[...] + bdiag_ref[...]
    y = _dot((u * s).astype(BF16), wout_ref[...])
    y_ref[...] = x + _rms(y, post_g_ref[...])


def _sgu_sample(x, pre_g, post_g, layer, win, sgug, wdiag, bdiag, wout):
    n, d = x.shape
    whole = lambda shape: pl.BlockSpec(shape, lambda i: (0,) * len(shape))
    return pl.pallas_call(
        _sgu_sample_kernel,
        grid=(1,),
        in_specs=[whole((n, d)), whole((1, d)), whole((1, d)), _layer_resident(win, layer),
                  whole((1, SGU_WIDTH)), whole((1, SGU_WIDTH)), whole((1, SGU_WIDTH)),
                  _layer_resident(wout, layer)],
        out_specs=[whole((n, d)), whole((n, SGU_WIDTH))],
        out_shape=[jax.ShapeDtypeStruct((n, d), F32), jax.ShapeDtypeStruct((n, SGU_WIDTH), F32)],
        compiler_params=_params(1),
        name="sgu_sample",
    )(x, pre_g, post_g, win, sgug, wdiag, bdiag, wout)


def _kv_side_kernel(x_ref, cs_ref, g_in_ref, wc_ref, wpe_ref, kvg_ref, ckv_ref, kr_ref, kcat_ref, ckvt_ref):
    h = _rms(x_ref[0], g_in_ref[...]).astype(BF16)
    ckv = _rms(_dot(h, wc_ref[...]), kvg_ref[...])
    t = _dot(h, wpe_ref[...]) * cs_ref[...]
    t = t + pltpu.roll(t, QK_ROPE, 1)
    lane = lax.broadcasted_iota(jnp.int32, t.shape, 1)
    krp = jnp.where(lane < QK_ROPE, t, 0.0)
    ckv_ref[0] = ckv
    kr_ref[0] = t[:, :QK_ROPE]
    kcat_ref[0] = jnp.concatenate([ckv, krp], axis=1).astype(BF16)
    ckvt_ref[0, 0] = ckv.T.astype(BF16)


def _kv_side(x, cs_tab, g_in, wc, wpe, kvg, tm):
    b, t, d = x.shape
    return pl.pallas_call(
        _kv_side_kernel,
        grid=(b, t // tm),
        in_specs=[
            pl.BlockSpec((1, tm, d), lambda bi, i: (bi, i, 0)),
            pl.BlockSpec((tm, LANES), lambda bi, i: (i, 0)),
            _resident((1, d)), _resident(wc.shape), _resident(wpe.shape), _resident((1, KV_LORA)),
        ],
        out_specs=[
            pl.BlockSpec((1, tm, KV_LORA), lambda bi, i: (bi, i, 0)),
            pl.BlockSpec((1, tm, QK_ROPE), lambda bi, i: (bi, i, 0)),
            pl.BlockSpec((1, tm, KCAT), lambda bi, i: (bi, i, 0)),
            pl.BlockSpec((1, 1, KV_LORA, tm), lambda bi, i: (bi, i, 0, 0)),
        ],
        out_shape=[
            jax.ShapeDtypeStruct((b, t, KV_LORA), F32),
            jax.ShapeDtypeStruct((b, t, QK_ROPE), F32),
            jax.ShapeDtypeStruct((b, t, KCAT), BF16),
            jax.ShapeDtypeStruct((b, t // tm, KV_LORA, tm), BF16),
        ],
        compiler_params=_params(2),
        name="kv_side",
    )(x, cs_tab, g_in, wc, wpe, kvg)


def _q_rows(x, cos, sin, pre_g, wdq, qg, wuq_n, wuq_pa, wuq_pb, wuk_t, store):
    h = _rms(x, pre_g).astype(BF16)
    cq = _rms(_dot(h, wdq[...]), qg).astype(BF16)
    qn = _dot(cq, wuq_n[...])
    qa = _dot(cq, wuq_pa[...])
    qb = _dot(cq, wuq_pb[...])
    for hd in range(N_HEADS):
        sl = slice(hd * LANES, (hd + 1) * LANES)
        qpe = qa[:, sl] * cos + qb[:, sl] * sin
        ql = _dot(qn[:, sl].astype(BF16), wuk_t[hd])
        store(hd, (ql * ATTN_SCALE).astype(BF16), (qpe * ATTN_SCALE).astype(BF16))


def _mla_out(o_heads, x, post_g, wuv, wo, o_ref):
    for hd in range(N_HEADS):
        o_ref[:, hd * V_HEAD:(hd + 1) * V_HEAD] = _dot(o_heads(hd).astype(BF16), wuv[hd]).astype(BF16)
    m = _dot(o_ref[...], wo[...])
    return x + _rms(m, post_g)


def _mla_prompt_kernel(x_ref, k_ref, vt_ref, cos_ref, sin_ref, pre_g_ref, post_g_ref, wdq_ref, qg_ref,
                       wuqn_ref, wuqa_ref, wuqb_ref, wuk_ref, wuvt_ref, wot_ref,
                       y_ref, qt_ref, m_ref, l_ref, acc_ref, o_ref):
    tq = x_ref.shape[1]
    qi = pl.program_id(1)
    x = x_ref[0]

    h = _rms(x, pre_g_ref[...]).astype(BF16)
    cq = _rms(_dot(h, wdq_ref[...]), qg_ref[...])
    cq_t = cq.T.astype(BF16)
    qn_t = _dot(wuqn_ref[...], cq_t)
    qa_t = _dot(wuqa_ref[...], cq_t)
    qb_t = _dot(wuqb_ref[...], cq_t)
    cos_t, sin_t = cos_ref[...], sin_ref[...]
    for hd in range(N_HEADS):
        cols = slice(hd * tq, (hd + 1) * tq)
        ql_t = _dot(wuk_ref[hd], qn_t[hd * QK_NOPE:(hd + 1) * QK_NOPE].astype(BF16))
        rows = slice(hd * QK_ROPE, (hd + 1) * QK_ROPE)
        qpe_t = qa_t[rows] * cos_t + qb_t[rows] * sin_t
        qt_ref[:KV_LORA, cols] = (ql_t * ATTN_SCALE).astype(BF16)
        qt_ref[KV_LORA:KV_LORA + QK_ROPE, cols] = (qpe_t * ATTN_SCALE).astype(BF16)
    qt_ref[KV_LORA + QK_ROPE:, :] = jnp.zeros((KCAT - KV_LORA - QK_ROPE, N_HEADS * tq), BF16)

    m_ref[...] = jnp.full(m_ref.shape, -jnp.inf, F32)
    l_ref[...] = jnp.zeros(l_ref.shape, F32)
    acc_ref[...] = jnp.zeros(acc_ref.shape, F32)

    def step(kb, diagonal):
        k = k_ref[0, pl.ds(pl.multiple_of(kb * tq, tq), tq), :]
        vt = vt_ref[0, kb]
        scores = [_dot(k, qt_ref[:, hd * tq:(hd + 1) * tq]) for hd in range(N_HEADS)]
        for hd in range(N_HEADS):
            cols = slice(hd * tq, (hd + 1) * tq)
            st = scores[hd]
            if diagonal:
                key = lax.broadcasted_iota(jnp.int32, st.shape, 0)
                qry = lax.broadcasted_iota(jnp.int32, st.shape, 1)
                st = jnp.where(key <= qry, st, -jnp.inf)
            m_old = m_ref[hd:hd + 1, :]
            m_new = jnp.maximum(m_old, jnp.max(st, axis=0, keepdims=True))
            alpha = jnp.exp(m_old - m_new)
            p = jnp.exp(st - m_new)
            l_ref[hd:hd + 1, :] = l_ref[hd:hd + 1, :] * alpha + jnp.sum(p, axis=0, keepdims=True)
            acc_ref[:, cols] = acc_ref[:, cols] * alpha + _dot(vt, p.astype(BF16))
            m_ref[hd:hd + 1, :] = m_new

    def body(kb, carry):
        step(kb, False)
        return carry

    lax.fori_loop(0, qi, body, 0)
    step(qi, True)

    for hd in range(N_HEADS):
        cols = slice(hd * tq, (hd + 1) * tq)
        o_t = acc_ref[:, cols] * (1.0 / l_ref[hd:hd + 1, :])
        o_ref[hd * V_HEAD:(hd + 1) * V_HEAD, :] = _dot(wuvt_ref[hd], o_t.astype(BF16)).astype(BF16)
    m_t = _dot(wot_ref[...], o_ref[...])
    y_ref[0] = x + _rms(m_t.T, post_g_ref[...])


def _mla_prompt(x, kcat, ckv_t, cos_t, sin_t, pre_g, post_g, wdq, qg, wuqn_t, wuqa_t, wuqb_t, wuk_h,
                wuvt_h, wo_t):
    b, t, d = x.shape
    tq = TQ
    lanes = N_HEADS * tq
    weights = (wdq, qg, wuqn_t, wuqa_t, wuqb_t, wuk_h, wuvt_h, wo_t)
    return pl.pallas_call(
        _mla_prompt_kernel,
        grid=(b, t // tq),
        in_specs=[
            pl.BlockSpec((1, tq, d), lambda bi, i: (bi, i, 0)),
            pl.BlockSpec((1, t, KCAT), lambda bi, i: (bi, 0, 0)),
            pl.BlockSpec((1, t // tq, KV_LORA, tq), lambda bi, i: (bi, 0, 0, 0)),
            pl.BlockSpec((QK_ROPE, tq), lambda bi, i: (0, i)),
            pl.BlockSpec((QK_ROPE, tq), lambda bi, i: (0, i)),
            _resident((1, d)), _resident((1, d)),
        ] + [_resident(w.shape) for w in weights],
        out_specs=pl.BlockSpec((1, tq, d), lambda bi, i: (bi, i, 0)),
        out_shape=jax.ShapeDtypeStruct((b, t, d), F32),
        scratch_shapes=[
            pltpu.VMEM((KCAT, lanes), BF16),
            pltpu.VMEM((N_HEADS, tq), F32), pltpu.VMEM((N_HEADS, tq), F32),
            pltpu.VMEM((KV_LORA, lanes), F32),
            pltpu.VMEM((N_HEADS * V_HEAD, tq), BF16),
        ],
        compiler_params=_params(2),
        name="mla_prompt",
    )(x, kcat, ckv_t, cos_t, sin_t, pre_g, post_g, *weights)


def _q_sample_kernel(x_ref, cos_ref, sin_ref, pre_g_ref, wdq_ref, qg_ref, wuqn_ref, wuqa_ref, wuqb_ref,
                     wukt_ref, q_ref):
    def store(hd, ql, qpe):
        q_ref[hd, :, :KV_LORA] = ql
        q_ref[hd, :, KV_LORA:] = qpe

    _q_rows(x_ref[...], cos_ref[...], sin_ref[...], pre_g_ref[...], wdq_ref, qg_ref[...],
            wuqn_ref, wuqa_ref, wuqb_ref, wukt_ref, store)


def _q_sample(x, cos, sin, pre_g, wdq, qg, wuq_n, wuq_pa, wuq_pb, wuk_t):
    n = x.shape[0]
    return pl.pallas_call(
        _q_sample_kernel,
        out_shape=jax.ShapeDtypeStruct((N_HEADS, n, KCAT), BF16),
        compiler_params=pltpu.CompilerParams(vmem_limit_bytes=VMEM_LIMIT),
        name="q_sample",
    )(x, cos, sin, pre_g, wdq, qg, wuq_n, wuq_pa, wuq_pb, wuk_t)


def _decode_kernel(pt_ref, q_ref, knew_ref, ckv_hbm, krt_hbm, o_ref, cbuf, rbuf, sem):
    b = pl.program_id(0)
    nb = pl.num_programs(0)
    n_pages = cbuf.shape[1]
    n_chunks = n_pages // DEC_CHUNK_PAGES
    cp_rows = DEC_CHUNK_PAGES * PAGE_SIZE
    ahead = DEC_SLOTS - 1

    def page_copies(page, slot, p):
        return (pltpu.make_async_copy(ckv_hbm.at[page], cbuf.at[slot, p], sem.at[0, slot]),
                pltpu.make_async_copy(krt_hbm.at[page], rbuf.at[slot, p], sem.at[1, slot]))

    def start_pages(seq, slot, p_lo, p_hi):
        for p in range(p_lo, p_hi):
            for cp in page_copies(pt_ref[seq, p], slot, p):
                cp.start(priority=p % 2)

    def wait_slot(slot):
        for p in range(n_pages):
            for cp in page_copies(0, slot, p):
                cp.wait()

    slot = b % DEC_SLOTS
    nxt = jnp.minimum(b + ahead, nb - 1)
    nxt_slot = (b + ahead) % DEC_SLOTS

    @pl.when(b == 0)
    def _():
        for s in range(ahead):
            start_pages(s, s, 0, n_pages)

    wait_slot(slot)

    q = q_ref[0]
    ql = q[:, :KV_LORA]
    qp = q[:, KV_LORA:KV_LORA + QK_ROPE]

    def chunk_scores(c):
        start_pages(nxt, nxt_slot, c * DEC_CHUNK_PAGES, (c + 1) * DEC_CHUNK_PAGES)
        s_pe = jnp.concatenate(
            [_dot(qp, rbuf[slot, c * DEC_CHUNK_PAGES + j].astype(BF16)) for j in range(DEC_CHUNK_PAGES)],
            axis=1)
        rows = cbuf[slot, c * DEC_CHUNK_PAGES:(c + 1) * DEC_CHUNK_PAGES].reshape(cp_rows, KV_LORA)
        return _dot(ql, rows.T.astype(BF16)) + s_pe, rows.astype(BF16)

    kn = knew_ref[0].astype(F32)
    m = jnp.sum(q.astype(F32) * kn, axis=-1, keepdims=True)
    l = jnp.ones_like(m)
    acc = jnp.broadcast_to(kn[:, :KV_LORA], (N_HEADS, KV_LORA))
    nxt_chunk = chunk_scores(0)
    for c in range(n_chunks):
        s, kc = nxt_chunk
        if c + 1 < n_chunks:
            nxt_chunk = chunk_scores(c + 1)
        m_new = jnp.maximum(m, jnp.max(s, axis=-1, keepdims=True))
        alpha = jnp.exp(m - m_new)
        p = jnp.exp(s - m_new)
        l = l * alpha + jnp.sum(p, axis=-1, keepdims=True)
        acc = acc * alpha + _dot(p.astype(BF16), kc)
        m = m_new
    o_ref[0] = acc / l

    @pl.when(b == nb - 1)
    def _():
        for s in range(1, DEC_SLOTS):
            wait_slot((slot + s) % DEC_SLOTS)


def _decode(page_table, q, knew, cache_ckv, cache_krt):
    n, n_pages = page_table.shape
    grid_spec = pltpu.PrefetchScalarGridSpec(
        num_scalar_prefetch=1,
        grid=(n,),
        in_specs=[
            pl.BlockSpec((1, N_HEADS, KCAT), lambda bi, pt: (bi, 0, 0)),
            pl.BlockSpec((1, 1, KCAT), lambda bi, pt: (bi, 0, 0)),
            pl.BlockSpec(memory_space=pl.ANY),
            pl.BlockSpec(memory_space=pl.ANY),
        ],
        out_specs=pl.BlockSpec((1, N_HEADS, KV_LORA), lambda bi, pt: (bi, 0, 0)),
        scratch_shapes=[
            pltpu.VMEM((DEC_SLOTS, n_pages, PAGE_SIZE, KV_LORA), F32),
            pltpu.VMEM((DEC_SLOTS, n_pages, QK_ROPE, PAGE_SIZE), F32),
            pltpu.SemaphoreType.DMA((2, DEC_SLOTS)),
        ],
    )
    return pl.pallas_call(
        _decode_kernel,
        grid_spec=grid_spec,
        out_shape=jax.ShapeDtypeStruct((n, N_HEADS, KV_LORA), F32),
        compiler_params=_params(1),
        name="mla_decode",
    )(page_table, q, knew, cache_ckv, cache_krt)


def _out_sample_kernel(o_ref, x_ref, post_g_ref, wuv_ref, wo_ref, y_ref, os_ref):
    y_ref[...] = _mla_out(lambda hd: o_ref[hd], x_ref[...], post_g_ref[...], wuv_ref, wo_ref, os_ref)


def _out_sample(o_heads, x, post_g, wuv, wo):
    n, d = x.shape
    return pl.pallas_call(
        _out_sample_kernel,
        out_shape=jax.ShapeDtypeStruct((n, d), F32),
        scratch_shapes=[pltpu.VMEM((n, N_HEADS * V_HEAD), BF16)],
        compiler_params=pltpu.CompilerParams(vmem_limit_bytes=VMEM_LIMIT),
        name="out_sample",
    )(o_heads, x, post_g, wuv, wo)


def _rot_half_cols(w):
    half = QK_ROPE // 2
    return jnp.concatenate([-w[..., half:], w[..., :half]], axis=-1)


def _rope_tables(pos):
    half = QK_ROPE // 2
    inv = 1.0 / (ROPE_THETA ** (jnp.arange(half, dtype=F32) / half))
    ang = pos.astype(F32)[:, None] * inv[None, :]
    cos, sin = jnp.cos(ang), jnp.sin(ang)
    cos64 = jnp.concatenate([cos, cos], axis=-1)
    sin64 = jnp.concatenate([sin, sin], axis=-1)
    return cos64, sin64


def kernel(x_prompt, x_sample, cache_ckv, cache_kr, state_conv, page_table, pre_mix_g, post_mix_g,
           pre_ffn_g, post_ffn_g, w_in_a, sgu_g, w_s, b_s, w_out_a, kv_in_g, w_dkv, kv_g, w_uk, w_uv,
           w_dq, q_g, w_uq, w_o, w_up, conv_w, conv_b, w_down):
    depth = w_up.shape[0]
    n_a = w_in_a.shape[0]
    bp, tp, _ = x_prompt.shape
    ns = x_sample.shape[0]
    past_len = page_table.shape[1] * PAGE_SIZE
    row = lambda v: v.reshape(1, -1)

    w_in_b, w_out_b = w_in_a.astype(BF16), w_out_a.astype(BF16)
    w_up_b, w_down_b = w_up.astype(BF16), w_down.astype(BF16)
    w_dq_b, w_o_b = w_dq.astype(BF16), w_o.astype(BF16)
    wuq = w_uq.reshape(-1, Q_LORA, N_HEADS, QK_NOPE + QK_ROPE)
    wuq_n = wuq[..., :QK_NOPE].reshape(-1, Q_LORA, N_HEADS * QK_NOPE).astype(BF16)
    pad = lambda w: jnp.pad(w, ((0, 0),) * 3 + ((0, LANES - QK_ROPE),)).reshape(
        -1, Q_LORA, N_HEADS * LANES).astype(BF16)
    wuq_pa = pad(wuq[..., QK_NOPE:])
    wuq_pb = pad(_rot_half_cols(wuq[..., QK_NOPE:]))
    wuk_t = jnp.transpose(w_uk, (1, 2, 0)).astype(BF16)
    wuv_h = jnp.transpose(w_uv, (1, 0, 2)).astype(BF16)
    wuqn_t = jnp.swapaxes(wuq_n, 1, 2)
    flat_t = lambda w: jnp.swapaxes(w.reshape(-1, Q_LORA, N_HEADS * QK_ROPE), 1, 2).astype(BF16)
    wuqa_t = flat_t(wuq[..., QK_NOPE:])
    wuqb_t = flat_t(_rot_half_cols(wuq[..., QK_NOPE:]))
    wuk_h = jnp.transpose(w_uk, (1, 0, 2)).astype(BF16)
    wuvt_h = jnp.transpose(w_uv, (1, 2, 0)).astype(BF16)
    wo_t = jnp.swapaxes(w_o_b, 1, 2)
    cache_krt = jnp.swapaxes(cache_kr, 1, 2)
    w_c = w_dkv[:, :KV_LORA].astype(BF16)
    w_pe = w_dkv[:, KV_LORA:]
    w_pe2 = jnp.concatenate([w_pe, _rot_half_cols(w_pe)], axis=-1).astype(BF16)
    bs_col = b_s[..., None]
    wdiag = jnp.repeat(w_s[:, :, 0, 0], SGU_GROUP_DIM, axis=-1)
    bdiag = jnp.repeat(b_s[:, :, 0], SGU_GROUP_DIM, axis=-1)

    cos_p, sin_p = _rope_tables(jnp.arange(tp, dtype=jnp.int32))
    cos_s, sin_s = _rope_tables(jnp.full((ns,), past_len, dtype=jnp.int32))
    tile2 = lambda a: jnp.concatenate([a, a], axis=-1)
    cs_p = jnp.concatenate([cos_p, sin_p], axis=-1)
    cs_s = jnp.concatenate([cos_s, sin_s], axis=-1)

    xp = x_prompt
    xs = x_sample.reshape(ns, D_MODEL)
    conv_p, conv_s, v_rows = [], [], []
    kcat_p = kcat_s = ckvt_p = ckv_p = kr_p = ckv_s = kr_s = None
    for layer in range(depth):
        if layer == n_a:
            ckv_p, kr_p, kcat_p, ckvt_p = _kv_side(xp, cs_p, row(kv_in_g), w_c, w_pe2, row(kv_g), TQ)
            ckv_s, kr_s, kcat_s, _ = _kv_side(xs[None], cs_s, row(kv_in_g), w_c, w_pe2, row(kv_g), ns)
        pre_g, post_g = row(pre_mix_g[layer]), row(post_mix_g[layer])
        if layer < n_a:
            xp = _sgu_prompt(xp, pre_g, post_g, layer, w_in_b, row(sgu_g[layer]), w_s[layer],
                             bs_col[layer], w_out_b)
            xs, v = _sgu_sample(xs, pre_g, post_g, layer, w_in_b, row(sgu_g[layer]), row(wdiag[layer]),
                                row(bdiag[layer]), w_out_b)
            v_rows.append(v)
        else:
            j = layer - n_a
            qw = (w_dq_b[j], row(q_g[j]), wuq_n[j], wuq_pa[j], wuq_pb[j], wuk_t)
            xp = _mla_prompt(xp, kcat_p, ckvt_p, cos_p.T, sin_p.T, pre_g, post_g, w_dq_b[j], row(q_g[j]),
                             wuqn_t[j], wuqa_t[j], wuqb_t[j], wuk_h, wuvt_h, wo_t[j])
            q = _q_sample(xs, tile2(cos_s), tile2(sin_s), pre_g, *qw)
            o = _decode(page_table, jnp.transpose(q, (1, 0, 2)), kcat_s.reshape(ns, 1, KCAT),
                        cache_ckv, cache_krt)
            xs = _out_sample(jnp.transpose(o, (1, 0, 2)), xs, post_g, wuv_h, w_o_b[j])
        pre_g, post_g = row(pre_ffn_g[layer]), row(post_ffn_g[layer])
        xp, tail = _ffn_prompt(xp, pre_g, post_g, layer, w_up_b, conv_w[layer], row(conv_b[layer]),
                               w_down_b)
        conv_p.append(tail)
        xs, st = _ffn_sample(xs, state_conv[layer], pre_g, post_g, layer, w_up_b, conv_w[layer],
                             row(conv_b[layer]), w_down_b)
        conv_s.append(st)

    return (xp, xs.reshape(ns, 1, D_MODEL), ckv_p, kr_p, jnp.stack(conv_p),
            ckv_s.reshape(ns, 1, KV_LORA), kr_s.reshape(ns, 1, QK_ROPE), jnp.stack(conv_s),
            jnp.stack(v_rows).reshape(n_a, ns, 1, SGU_WIDTH))
```

```python
import jax
import jax.numpy as jnp
from jax import lax
from jax.experimental import pallas as pl
from jax.experimental.pallas import tpu as pltpu

F32 = jnp.float32
BF16 = jnp.bfloat16

D_MODEL = 1024
N_A_LAYERS = 2
CHUNK = 128
SGU_WIDTH = 2 * D_MODEL
SGU_GROUPS = 8
SGU_GROUP_DIM = SGU_WIDTH // SGU_GROUPS
N_HEADS = 8
QK_NOPE = 128
QK_ROPE = 64
V_HEAD = 128
Q_LORA = D_MODEL // 2
KV_LORA = D_MODEL // 4
ROPE_THETA = 10000.0
ATTN_SCALE = (QK_NOPE + QK_ROPE) ** -0.5
D_FF = 11 * D_MODEL // 4
CONV_W = 3
EPS = 1e-6
PAGE_SIZE = 128

LANES = 128
BF16_ROWS = 16
VMEM_LIMIT = 56 * 1024 * 1024

KCAT = KV_LORA + LANES
KHEAD = QK_NOPE + LANES
FF_TILE = 256
N_FF = D_FF // FF_TILE
TM_FFN = 512
TM_SGU = 512
TQ = 256
DEC_CHUNK_PAGES = 8
DEC_SLOTS = 3


def _rms(x, g):
    return x * lax.rsqrt(jnp.mean(x * x, axis=-1, keepdims=True) + EPS) * g


def _gelu_erf(x):
    return 0.5 * x * (1.0 + lax.erf(x * (0.5 ** 0.5)))


def _dot(a, b):
    return jnp.dot(a, b, preferred_element_type=F32)


def _dot_nt(a, b):
    return lax.dot_general(a, b, (((1,), (1,)), ((), ())), preferred_element_type=F32)


def _params(n_grid_axes):
    return pltpu.CompilerParams(
        dimension_semantics=("arbitrary",) * n_grid_axes, vmem_limit_bytes=VMEM_LIMIT)


def _resident(shape):
    nd = len(shape)
    return pl.BlockSpec(shape, lambda *_: (0,) * nd, pipeline_mode=pl.Buffered(1))


def _layer_resident(stacked, layer):
    nd = stacked.ndim - 1
    return pl.BlockSpec((None,) + stacked.shape[1:], lambda *_: (layer,) + (0,) * nd,
                        pipeline_mode=pl.Buffered(1))


def _ffn_prompt_kernel(x_ref, pre_g_ref, post_g_ref, wup_ref, cw_ref, cb_ref, wdn_ref,
                       y_ref, tail_ref, act_ref, h_ref):
    tm = x_ref.shape[1]
    ng = tm // 8
    i = pl.program_id(1)

    @pl.when(i == 0)
    def _():
        tail_ref[...] = jnp.zeros(tail_ref.shape, F32)

    @pl.when(i >= 0)
    def _():
        hn = _rms(x_ref[0], pre_g_ref[...])
        h_ref[...] = jnp.swapaxes(hn.reshape(8, ng, hn.shape[-1]), 0, 1).reshape(tm, -1).astype(BF16)

    first = lax.broadcasted_iota(jnp.int32, (8, FF_TILE), 0) == 0

    def conv(col0):
        sl = slice(col0, col0 + FF_TILE)
        ap = _dot(h_ref[...], wup_ref[:, sl])
        prev = tail_ref[0, :, sl]
        sh1 = jnp.where(first, pltpu.roll(prev[8:], 1, 0), pltpu.roll(ap[tm - 8:], 1, 0))
        sh2 = jnp.where(first, pltpu.roll(prev[:8], 1, 0), pltpu.roll(ap[tm - 16:tm - 8], 1, 0))
        w = cw_ref[:, sl]
        c = cb_ref[:, sl] + jnp.concatenate([sh2, sh1, ap[:tm - 16]], axis=0) * w[0:1]
        c = c + jnp.concatenate([sh1, ap[:tm - 8]], axis=0) * w[1:2]
        c = c + ap * w[2:3]
        tail_ref[0, :, sl] = ap[tm - 16:]
        return c

    for f in range(N_FF):
        cg = conv(f * FF_TILE)
        cu = conv(D_FF + f * FF_TILE)
        act_ref[:, f * FF_TILE:(f + 1) * FF_TILE] = (jax.nn.gelu(cg, approximate=True) * cu).astype(BF16)
    fo = _rms(_dot(act_ref[...], wdn_ref[...]), post_g_ref[...])
    y_ref[0] = x_ref[0] + jnp.swapaxes(fo.reshape(ng, 8, fo.shape[-1]), 0, 1).reshape(tm, -1)


def _ffn_prompt(x, pre_g, post_g, layer, wup, cw, cb, wdn):
    b, t, d = x.shape
    tm = TM_FFN
    y, tail = pl.pallas_call(
        _ffn_prompt_kernel,
        grid=(b, t // tm),
        in_specs=[
            pl.BlockSpec((1, tm, d), lambda bi, i: (bi, i, 0)),
            _resident((1, d)), _resident((1, d)),
            _layer_resident(wup, layer), _resident(cw.shape), _resident(cb.shape),
            _layer_resident(wdn, layer),
        ],
        out_specs=[
            pl.BlockSpec((1, tm, d), lambda bi, i: (bi, i, 0)),
            pl.BlockSpec((1, 16, 2 * D_FF), lambda bi, i: (bi, 0, 0)),
        ],
        out_shape=[
            jax.ShapeDtypeStruct((b, t, d), F32),
            jax.ShapeDtypeStruct((b, 16, 2 * D_FF), F32),
        ],
        scratch_shapes=[pltpu.VMEM((tm, D_FF), BF16), pltpu.VMEM((tm, d), BF16)],
        compiler_params=_params(2),
        name="ffn_prompt",
    )(x, pre_g, post_g, wup, cw, cb, wdn)
    return y, tail[:, 7::8, :]


def _ffn_sample_kernel(x_ref, p0g_ref, p0u_ref, p1g_ref, p1u_ref, pre_g_ref, post_g_ref,
                       wg_ref, wu_ref, cwg_ref, cwu_ref, cbg_ref, cbu_ref, wdn_ref,
                       y_ref, ag_ref, au_ref, h_ref, acc_ref):
    f = pl.program_id(0)

    @pl.when(f == 0)
    def _():
        h_ref[...] = _rms(x_ref[...], pre_g_ref[...]).astype(BF16)

    h = h_ref[...]

    def conv(w_ref, p0_ref, p1_ref, cw_ref, cb_ref, a_ref):
        a = _dot(h, w_ref[...])
        a_ref[...] = a
        w = cw_ref[...]
        c = cb_ref[...] + p0_ref[...] * w[0:1]
        c = c + p1_ref[...] * w[1:2]
        return c + a * w[2:3]

    cg = conv(wg_ref, p0g_ref, p1g_ref, cwg_ref, cbg_ref, ag_ref)
    cu = conv(wu_ref, p0u_ref, p1u_ref, cwu_ref, cbu_ref, au_ref)
    act = (jax.nn.gelu(cg, approximate=True) * cu).astype(BF16)
    part = _dot(act, wdn_ref[...])

    @pl.when(f == 0)
    def _():
        acc_ref[...] = part

    @pl.when(f > 0)
    def _():
        acc_ref[...] += part

    @pl.when(f == pl.num_programs(0) - 1)
    def _():
        y_ref[...] = x_ref[...] + _rms(acc_ref[...], post_g_ref[...])


def _ffn_sample(x, state, pre_g, post_g, layer, wup, cw, cb, wdn):
    n, d = x.shape
    tf = FF_TILE
    st = state.reshape(n, (CONV_W - 1) * 2 * D_FF)
    hist = 2 * D_FF // tf
    col = lambda off: pl.BlockSpec((n, tf), lambda f: (0, off + f))
    row3 = lambda off: pl.BlockSpec((CONV_W, tf), lambda f: (0, off + f))
    row1 = lambda off: pl.BlockSpec((1, tf), lambda f: (0, off + f))
    const = lambda shape: pl.BlockSpec(shape, lambda f: (0,) * len(shape))
    y, ag, au = pl.pallas_call(
        _ffn_sample_kernel,
        grid=(N_FF,),
        in_specs=[
            const((n, d)),
            col(0), col(N_FF), col(hist), col(hist + N_FF),
            const((1, d)), const((1, d)),
            pl.BlockSpec((None, d, tf), lambda f: (layer, 0, f)),
            pl.BlockSpec((None, d, tf), lambda f: (layer, 0, N_FF + f)),
            row3(0), row3(N_FF), row1(0), row1(N_FF),
            pl.BlockSpec((None, tf, d), lambda f: (layer, f, 0)),
        ],
        out_specs=[const((n, d)), col(0), col(0)],
        out_shape=[
            jax.ShapeDtypeStruct((n, d), F32),
            jax.ShapeDtypeStruct((n, D_FF), F32),
            jax.ShapeDtypeStruct((n, D_FF), F32),
        ],
        scratch_shapes=[pltpu.VMEM((n, d), BF16), pltpu.VMEM((n, d), F32)],
        compiler_params=_params(1),
        name="ffn_sample",
    )(x, st, st, st, st, pre_g, post_g, wup, wup, cw, cw, cb, cb, wdn)
    new_state = jnp.stack([state[:, 1, :], jnp.concatenate([ag, au], axis=-1)], axis=1)
    return y, new_state


def _sgu_prompt_kernel(x_ref, pre_g_ref, post_g_ref, win_ref, sgug_ref, ws_ref, bs_ref, wout_ref,
                       y_ref, v_ref, gated_ref, h_ref):
    tm = x_ref.shape[1]
    gd = SGU_GROUP_DIM
    x = x_ref[0]

    @pl.when(pl.program_id(1) >= 0)
    def _():
        h_ref[...] = _rms(x, pre_g_ref[...]).astype(BF16)

    ss = jnp.zeros((tm, 1), F32)
    for j in range(SGU_GROUPS):
        zc = _gelu_erf(_dot(h_ref[...], win_ref[:, SGU_WIDTH + j * gd:SGU_WIDTH + (j + 1) * gd]))
        v_ref[:, j * gd:(j + 1) * gd] = zc
        ss = ss + jnp.sum(zc * zc, axis=-1, keepdims=True)
    inv = lax.rsqrt(ss * (1.0 / SGU_WIDTH) + EPS)
    causal = (lax.broadcasted_iota(jnp.int32, (CHUNK, CHUNK), 0)
              >= lax.broadcasted_iota(jnp.int32, (CHUNK, CHUNK), 1))
    for g in range(SGU_GROUPS):
        sl = slice(g * gd, (g + 1) * gd)
        vn = (v_ref[:, sl] * inv * sgug_ref[:, sl]).astype(BF16)
        wm = jnp.where(causal, ws_ref[g], 0.0).astype(BF16)
        bias = bs_ref[g]
        s = jnp.concatenate(
            [_dot(wm, vn[c * CHUNK:(c + 1) * CHUNK]) + bias for c in range(tm // CHUNK)], axis=0)
        u = _gelu_erf(_dot(h_ref[...], win_ref[:, sl]))
        gated_ref[:, sl] = (u * s).astype(BF16)
    y_ref[0] = x + _rms(_dot(gated_ref[...], wout_ref[...]), post_g_ref[...])


def _sgu_prompt(x, pre_g, post_g, layer, win, sgug, ws, bs, wout):
    b, t, d = x.shape
    tm = TM_SGU
    return pl.pallas_call(
        _sgu_prompt_kernel,
        grid=(b, t // tm),
        in_specs=[
            pl.BlockSpec((1, tm, d), lambda bi, i: (bi, i, 0)),
            _resident((1, d)), _resident((1, d)),
            _layer_resident(win, layer), _resident(sgug.shape), _resident(ws.shape), _resident(bs.shape),
            _layer_resident(wout, layer),
        ],
        out_specs=pl.BlockSpec((1, tm, d), lambda bi, i: (bi, i, 0)),
        out_shape=jax.ShapeDtypeStruct((b, t, d), F32),
        scratch_shapes=[pltpu.VMEM((tm, SGU_WIDTH), F32), pltpu.VMEM((tm, SGU_WIDTH), BF16),
                        pltpu.VMEM((tm, d), BF16)],
        compiler_params=_params(2),
        name="sgu_prompt",
    )(x, pre_g, post_g, win, sgug, ws, bs, wout)


def _sgu_sample_kernel(x_ref, pre_g_ref, post_g_ref, win_ref, sgug_ref, wdiag_ref, bdiag_ref, wout_ref,
                       y_ref, v_ref):
    x = x_ref[...]
    h = _rms(x, pre_g_ref[...]).astype(BF16)
    u = _gelu_erf(_dot(h, win_ref[:, :SGU_WIDTH]))
    v = _gelu_erf(_dot(h, win_ref[:, SGU_WIDTH:]))
    vn = _rms(v, sgug_ref[...])
    v_ref[...] = vn
    s = vn * wdiag_ref[...] + bdiag_ref[...]
    y = _dot((u * s).astype(BF16), wout_ref[...])
    y_ref[...] = x + _rms(y, post_g_ref[...])


def _sgu_sample(x, pre_g, post_g, layer, win, sgug, wdiag, bdiag, wout):
    n, d = x.shape
    whole = lambda shape: pl.BlockSpec(shape, lambda i: (0,) * len(shape))
    return pl.pallas_call(
        _sgu_sample_kernel,
        grid=(1,),
        in_specs=[whole((n, d)), whole((1, d)), whole((1, d)), _layer_resident(win, layer),
                  whole((1, SGU_WIDTH)), whole((1, SGU_WIDTH)), whole((1, SGU_WIDTH)),
                  _layer_resident(wout, layer)],
        out_specs=[whole((n, d)), whole((n, SGU_WIDTH))],
        out_shape=[jax.ShapeDtypeStruct((n, d), F32), jax.ShapeDtypeStruct((n, SGU_WIDTH), F32)],
        compiler_params=_params(1),
        name="sgu_sample",
    )(x, pre_g, post_g, win, sgug, wdiag, bdiag, wout)


def _kv_side_kernel(x_ref, cs_ref, g_in_ref, wc_ref, wpe_ref, kvg_ref, wuk_ref,
                    ckv_ref, kr_ref, kcat_ref, ckvt_ref, khead_ref):
    h = _rms(x_ref[0], g_in_ref[...]).astype(BF16)
    ckv = _rms(_dot(h, wc_ref[...]), kvg_ref[...])
    t = _dot(h, wpe_ref[...]) * cs_ref[...]
    t = t + pltpu.roll(t, QK_ROPE, 1)
    lane = lax.broadcasted_iota(jnp.int32, t.shape, 1)
    krp = jnp.where(lane < QK_ROPE, t, 0.0)
    ckv_ref[0] = ckv
    kr_ref[0] = t.T[:QK_ROPE]
    kcat_ref[0] = jnp.concatenate([ckv, krp], axis=1).astype(BF16)
    ckvt_ref[0, 0] = ckv.T.astype(BF16)
    kn = _dot(ckv.astype(BF16), wuk_ref[...])
    for hd in range(N_HEADS):
        khead_ref[0, hd] = jnp.concatenate(
            [kn[:, hd * QK_NOPE:(hd + 1) * QK_NOPE], krp], axis=1).astype(BF16)


def _kv_side(x, cs_tab, g_in, wc, wpe, kvg, wuk, tm):
    b, t, d = x.shape
    return pl.pallas_call(
        _kv_side_kernel,
        grid=(b, t // tm),
        in_specs=[
            pl.BlockSpec((1, tm, d), lambda bi, i: (bi, i, 0)),
            pl.BlockSpec((tm, LANES), lambda bi, i: (i, 0)),
            _resident((1, d)), _resident(wc.shape), _resident(wpe.shape), _resident((1, KV_LORA)),
            _resident(wuk.shape),
        ],
        out_specs=[
            pl.BlockSpec((1, tm, KV_LORA), lambda bi, i: (bi, i, 0)),
            pl.BlockSpec((1, QK_ROPE, tm), lambda bi, i: (bi, 0, i)),
            pl.BlockSpec((1, tm, KCAT), lambda bi, i: (bi, i, 0)),
            pl.BlockSpec((1, 1, KV_LORA, tm), lambda bi, i: (bi, i, 0, 0)),
            pl.BlockSpec((1, N_HEADS, tm, KHEAD), lambda bi, i: (bi, 0, i, 0)),
        ],
        out_shape=[
            jax.ShapeDtypeStruct((b, t, KV_LORA), F32),
            jax.ShapeDtypeStruct((b, QK_ROPE, t), F32),
            jax.ShapeDtypeStruct((b, t, KCAT), BF16),
            jax.ShapeDtypeStruct((b, t // tm, KV_LORA, tm), BF16),
            jax.ShapeDtypeStruct((b, N_HEADS, t, KHEAD), BF16),
        ],
        compiler_params=_params(2),
        name="kv_side",
    )(x, cs_tab, g_in, wc, wpe, kvg, wuk)


def _q_rows(x, cos, sin, pre_g, wdq, qg, wuq_n, wuq_pa, wuq_pb, wuk_t, store):
    h = _rms(x, pre_g).astype(BF16)
    cq = _rms(_dot(h, wdq[...]), qg).astype(BF16)
    qn = _dot(cq, wuq_n[...])
    qa = _dot(cq, wuq_pa[...])
    qb = _dot(cq, wuq_pb[...])
    for hd in range(N_HEADS):
        sl = slice(hd * LANES, (hd + 1) * LANES)
        qpe = qa[:, sl] * cos + qb[:, sl] * sin
        ql = _dot(qn[:, sl].astype(BF16), wuk_t[hd])
        store(hd, (ql * ATTN_SCALE).astype(BF16), (qpe * ATTN_SCALE).astype(BF16))


def _mla_out(o_heads, x, post_g, wuv, wo, o_ref):
    for hd in range(N_HEADS):
        o_ref[:, hd * V_HEAD:(hd + 1) * V_HEAD] = _dot(o_heads(hd).astype(BF16), wuv[hd]).astype(BF16)
    m = _dot(o_ref[...], wo[...])
    return x + _rms(m, post_g)


def _mla_prompt_kernel(x_ref, k_ref, vt_ref, cos_ref, sin_ref, pre_g_ref, post_g_ref, wdq_ref, qg_ref,
                       wuqn_ref, wuqa_ref, wuqb_ref, wuvt_ref, wot_ref,
                       y_ref, qt_ref, m_ref, l_ref, acc_ref, o_ref):
    tq = x_ref.shape[1]
    qi = pl.program_id(1)
    x = x_ref[0]

    h = _rms(x, pre_g_ref[...]).astype(BF16)
    cq = _rms(_dot(h, wdq_ref[...]), qg_ref[...])
    cq_t = cq.T.astype(BF16)
    qn_t = _dot(wuqn_ref[...], cq_t)
    qa_t = _dot(wuqa_ref[...], cq_t)
    qb_t = _dot(wuqb_ref[...], cq_t)
    cos_t, sin_t = cos_ref[...], sin_ref[...]
    for hd in range(N_HEADS):
        cols = slice(hd * tq, (hd + 1) * tq)
        rows = slice(hd * QK_ROPE, (hd + 1) * QK_ROPE)
        qpe_t = qa_t[rows] * cos_t + qb_t[rows] * sin_t
        qt_ref[:QK_NOPE, cols] = (qn_t[hd * QK_NOPE:(hd + 1) * QK_NOPE] * ATTN_SCALE).astype(BF16)
        qt_ref[QK_NOPE:QK_NOPE + QK_ROPE, cols] = (qpe_t * ATTN_SCALE).astype(BF16)
    qt_ref[QK_NOPE + QK_ROPE:, :] = jnp.zeros((KHEAD - QK_NOPE - QK_ROPE, N_HEADS * tq), BF16)

    def step(kb, diagonal):
        keys = pl.ds(pl.multiple_of(kb * tq, tq), tq)
        vt = vt_ref[0, kb]
        scores = [_dot(k_ref[0, hd, keys, :], qt_ref[:, hd * tq:(hd + 1) * tq])
                  for hd in range(N_HEADS)]
        for hd in range(N_HEADS):
            cols = slice(hd * tq, (hd + 1) * tq)
            st = scores[hd]
            if diagonal:
                key = lax.broadcasted_iota(jnp.int32, st.shape, 0)
                qry = lax.broadcasted_iota(jnp.int32, st.shape, 1)
                st = jnp.where(key <= qry, st, -jnp.inf)
                m_new = jnp.max(st, axis=0, keepdims=True)
                p = jnp.exp(st - m_new)
                l_ref[hd:hd + 1, :] = jnp.sum(p, axis=0, keepdims=True)
                acc_ref[:, cols] = _dot(vt, p.astype(BF16))
            else:
                m_old = m_ref[hd:hd + 1, :]
                m_new = jnp.maximum(m_old, jnp.max(st, axis=0, keepdims=True))
                alpha = jnp.exp(m_old - m_new)
                p = jnp.exp(st - m_new)
                l_ref[hd:hd + 1, :] = l_ref[hd:hd + 1, :] * alpha + jnp.sum(p, axis=0, keepdims=True)
                acc_ref[:, cols] = acc_ref[:, cols] * alpha + _dot(vt, p.astype(BF16))
            m_ref[hd:hd + 1, :] = m_new

    def body(kb, carry):
        step(kb, False)
        return carry

    step(qi, True)
    lax.fori_loop(0, qi, body, 0)

    for hd in range(N_HEADS):
        cols = slice(hd * tq, (hd + 1) * tq)
        o_t = acc_ref[:, cols] * (1.0 / l_ref[hd:hd + 1, :])
        o_ref[hd * V_HEAD:(hd + 1) * V_HEAD, :] = _dot(wuvt_ref[hd], o_t.astype(BF16)).astype(BF16)
    m_t = _dot(wot_ref[...], o_ref[...])
    y_ref[0] = x + _rms(m_t.T, post_g_ref[...])


def _mla_prompt(x, khead, ckv_t, cos_t, sin_t, pre_g, post_g, wdq, qg, wuqn_t, wuqa_t, wuqb_t,
                wuvt_h, wo_t):
    b, t, d = x.shape
    tq = TQ
    lanes = N_HEADS * tq
    weights = (wdq, qg, wuqn_t, wuqa_t, wuqb_t, wuvt_h, wo_t)
    return pl.pallas_call(
        _mla_prompt_kernel,
        grid=(b, t // tq),
        in_specs=[
            pl.BlockSpec((1, tq, d), lambda bi, i: (bi, i, 0)),
            pl.BlockSpec((1, N_HEADS, t, KHEAD), lambda bi, i: (bi, 0, 0, 0)),
            pl.BlockSpec((1, t // tq, KV_LORA, tq), lambda bi, i: (bi, 0, 0, 0)),
            pl.BlockSpec((QK_ROPE, tq), lambda bi, i: (0, i)),
            pl.BlockSpec((QK_ROPE, tq), lambda bi, i: (0, i)),
            _resident((1, d)), _resident((1, d)),
        ] + [_resident(w.shape) for w in weights],
        out_specs=pl.BlockSpec((1, tq, d), lambda bi, i: (bi, i, 0)),
        out_shape=jax.ShapeDtypeStruct((b, t, d), F32),
        scratch_shapes=[
            pltpu.VMEM((KHEAD, lanes), BF16),
            pltpu.VMEM((N_HEADS, tq), F32), pltpu.VMEM((N_HEADS, tq), F32),
            pltpu.VMEM((KV_LORA, lanes), F32),
            pltpu.VMEM((N_HEADS * V_HEAD, tq), BF16),
        ],
        compiler_params=_params(2),
        name="mla_prompt",
    )(x, khead, ckv_t, cos_t, sin_t, pre_g, post_g, *weights)


def _q_sample_kernel(x_ref, cos_ref, sin_ref, pre_g_ref, wdq_ref, qg_ref, wuqn_ref, wuqa_ref, wuqb_ref,
                     wukt_ref, q_ref):
    def store(hd, ql, qpe):
        q_ref[hd, :, :KV_LORA] = ql
        q_ref[hd, :, KV_LORA:] = qpe

    _q_rows(x_ref[...], cos_ref[...], sin_ref[...], pre_g_ref[...], wdq_ref, qg_ref[...],
            wuqn_ref, wuqa_ref, wuqb_ref, wukt_ref, store)


def _q_sample(x, cos, sin, pre_g, wdq, qg, wuq_n, wuq_pa, wuq_pb, wuk_t):
    n = x.shape[0]
    return pl.pallas_call(
        _q_sample_kernel,
        out_shape=jax.ShapeDtypeStruct((N_HEADS, n, KCAT), BF16),
        compiler_params=pltpu.CompilerParams(vmem_limit_bytes=VMEM_LIMIT),
        name="q_sample",
    )(x, cos, sin, pre_g, wdq, qg, wuq_n, wuq_pa, wuq_pb, wuk_t)


def _decode_kernel(pt_ref, q_ref, knew_ref, ckv_hbm, krt_hbm, o_ref, cbuf, rbuf, sem):
    b = pl.program_id(0)
    nb = pl.num_programs(0)
    n_pages = cbuf.shape[1]
    n_chunks = n_pages // DEC_CHUNK_PAGES
    cp_rows = DEC_CHUNK_PAGES * PAGE_SIZE
    ahead = DEC_SLOTS - 1

    def page_copies(page, slot, p):
        return (pltpu.make_async_copy(ckv_hbm.at[page], cbuf.at[slot, p], sem.at[0, slot]),
                pltpu.make_async_copy(krt_hbm.at[page], rbuf.at[slot, p], sem.at[1, slot]))

    def start_pages(seq, slot, p_lo, p_hi):
        for p in range(p_lo, p_hi):
            for cp in page_copies(pt_ref[seq, p], slot, p):
                cp.start(priority=p % 2)

    def wait_slot(slot):
        for p in range(n_pages):
            for cp in page_copies(0, slot, p):
                cp.wait()

    slot = b % DEC_SLOTS
    nxt = jnp.minimum(b + ahead, nb - 1)
    nxt_slot = (b + ahead) % DEC_SLOTS

    @pl.when(b == 0)
    def _():
        for s in range(ahead):
            start_pages(s, s, 0, n_pages)

    wait_slot(slot)

    q = q_ref[0]
    ql = q[:, :KV_LORA]
    qp = q[:, KV_LORA:KV_LORA + QK_ROPE]

    def chunk_scores(c):
        start_pages(nxt, nxt_slot, c * DEC_CHUNK_PAGES, (c + 1) * DEC_CHUNK_PAGES)
        s_pe = jnp.concatenate(
            [_dot(qp, rbuf[slot, c * DEC_CHUNK_PAGES + j].astype(BF16)) for j in range(DEC_CHUNK_PAGES)],
            axis=1)
        rows = cbuf[slot, c * DEC_CHUNK_PAGES:(c + 1) * DEC_CHUNK_PAGES].reshape(cp_rows, KV_LORA)
        return _dot(ql, rows.T.astype(BF16)) + s_pe, rows.astype(BF16)

    kn = knew_ref[0].astype(F32)
    m = jnp.sum(q.astype(F32) * kn, axis=-1, keepdims=True)
    l = jnp.ones_like(m)
    acc = jnp.broadcast_to(kn[:, :KV_LORA], (N_HEADS, KV_LORA))
    nxt_chunk = chunk_scores(0)
    for c in range(n_chunks):
        s, kc = nxt_chunk
        if c + 1 < n_chunks:
            nxt_chunk = chunk_scores(c + 1)
        m_new = jnp.maximum(m, jnp.max(s, axis=-1, keepdims=True))
        alpha = jnp.exp(m - m_new)
        p = jnp.exp(s - m_new)
        l = l * alpha + jnp.sum(p, axis=-1, keepdims=True)
        acc = acc * alpha + _dot(p.astype(BF16), kc)
        m = m_new
    o_ref[0] = acc / l

    @pl.when(b == nb - 1)
    def _():
        for s in range(1, DEC_SLOTS):
            wait_slot((slot + s) % DEC_SLOTS)


def _decode(page_table, q, knew, cache_ckv, cache_krt):
    n, n_pages = page_table.shape
    grid_spec = pltpu.PrefetchScalarGridSpec(
        num_scalar_prefetch=1,
        grid=(n,),
        in_specs=[
            pl.BlockSpec((1, N_HEADS, KCAT), lambda bi, pt: (bi, 0, 0)),
            pl.BlockSpec((1, 1, KCAT), lambda bi, pt: (bi, 0, 0)),
            pl.BlockSpec(memory_space=pl.ANY),
            pl.BlockSpec(memory_space=pl.ANY),
        ],
        out_specs=pl.BlockSpec((1, N_HEADS, KV_LORA), lambda bi, pt: (bi, 0, 0)),
        scratch_shapes=[
            pltpu.VMEM((DEC_SLOTS, n_pages, PAGE_SIZE, KV_LORA), F32),
            pltpu.VMEM((DEC_SLOTS, n_pages, QK_ROPE, PAGE_SIZE), F32),
            pltpu.SemaphoreType.DMA((2, DEC_SLOTS)),
        ],
    )
    return pl.pallas_call(
        _decode_kernel,
        grid_spec=grid_spec,
        out_shape=jax.ShapeDtypeStruct((n, N_HEADS, KV_LORA), F32),
        compiler_params=_params(1),
        name="mla_decode",
    )(page_table, q, knew, cache_ckv, cache_krt)


def _out_sample_kernel(o_ref, x_ref, post_g_ref, wuv_ref, wo_ref, y_ref, os_ref):
    y_ref[...] = _mla_out(lambda hd: o_ref[hd], x_ref[...], post_g_ref[...], wuv_ref, wo_ref, os_ref)


def _out_sample(o_heads, x, post_g, wuv, wo):
    n, d = x.shape
    return pl.pallas_call(
        _out_sample_kernel,
        out_shape=jax.ShapeDtypeStruct((n, d), F32),
        scratch_shapes=[pltpu.VMEM((n, N_HEADS * V_HEAD), BF16)],
        compiler_params=pltpu.CompilerParams(vmem_limit_bytes=VMEM_LIMIT),
        name="out_sample",
    )(o_heads, x, post_g, wuv, wo)


def _rot_half_cols(w):
    half = QK_ROPE // 2
    return jnp.concatenate([-w[..., half:], w[..., :half]], axis=-1)


def _rope_tables(pos):
    half = QK_ROPE // 2
    inv = 1.0 / (ROPE_THETA ** (jnp.arange(half, dtype=F32) / half))
    ang = pos.astype(F32)[:, None] * inv[None, :]
    cos, sin = jnp.cos(ang), jnp.sin(ang)
    cos64 = jnp.concatenate([cos, cos], axis=-1)
    sin64 = jnp.concatenate([sin, sin], axis=-1)
    return cos64, sin64


def kernel(x_prompt, x_sample, cache_ckv, cache_kr, state_conv, page_table, pre_mix_g, post_mix_g,
           pre_ffn_g, post_ffn_g, w_in_a, sgu_g, w_s, b_s, w_out_a, kv_in_g, w_dkv, kv_g, w_uk, w_uv,
           w_dq, q_g, w_uq, w_o, w_up, conv_w, conv_b, w_down):
    depth = w_up.shape[0]
    n_a = w_in_a.shape[0]
    bp, tp, _ = x_prompt.shape
    ns = x_sample.shape[0]
    past_len = page_table.shape[1] * PAGE_SIZE
    row = lambda v: v.reshape(1, -1)

    w_in_b, w_out_b = w_in_a.astype(BF16), w_out_a.astype(BF16)
    w_up_b, w_down_b = w_up.astype(BF16), w_down.astype(BF16)
    w_dq_b, w_o_b = w_dq.astype(BF16), w_o.astype(BF16)
    wuq = w_uq.reshape(-1, Q_LORA, N_HEADS, QK_NOPE + QK_ROPE)
    wuq_n = wuq[..., :QK_NOPE].reshape(-1, Q_LORA, N_HEADS * QK_NOPE).astype(BF16)
    pad = lambda w: jnp.pad(w, ((0, 0),) * 3 + ((0, LANES - QK_ROPE),)).reshape(
        -1, Q_LORA, N_HEADS * LANES).astype(BF16)
    wuq_pa = pad(wuq[..., QK_NOPE:])
    wuq_pb = pad(_rot_half_cols(wuq[..., QK_NOPE:]))
    wuk_t = jnp.transpose(w_uk, (1, 2, 0)).astype(BF16)
    wuv_h = jnp.transpose(w_uv, (1, 0, 2)).astype(BF16)
    wuqn_t = jnp.swapaxes(wuq_n, 1, 2)
    flat_t = lambda w: jnp.swapaxes(w.reshape(-1, Q_LORA, N_HEADS * QK_ROPE), 1, 2).astype(BF16)
    wuqa_t = flat_t(wuq[..., QK_NOPE:])
    wuqb_t = flat_t(_rot_half_cols(wuq[..., QK_NOPE:]))
    wuk_all = w_uk.reshape(KV_LORA, N_HEADS * QK_NOPE).astype(BF16)
    wuvt_h = jnp.transpose(w_uv, (1, 2, 0)).astype(BF16)
    wo_t = jnp.swapaxes(w_o_b, 1, 2)
    cache_krt = jnp.swapaxes(cache_kr, 1, 2)
    w_c = w_dkv[:, :KV_LORA].astype(BF16)
    w_pe = w_dkv[:, KV_LORA:]
    w_pe2 = jnp.concatenate([w_pe, _rot_half_cols(w_pe)], axis=-1).astype(BF16)
    bs_col = b_s[..., None]
    wdiag = jnp.repeat(w_s[:, :, 0, 0], SGU_GROUP_DIM, axis=-1)
    bdiag = jnp.repeat(b_s[:, :, 0], SGU_GROUP_DIM, axis=-1)

    cos_p, sin_p = _rope_tables(jnp.arange(tp, dtype=jnp.int32))
    cos_s, sin_s = _rope_tables(jnp.full((ns,), past_len, dtype=jnp.int32))
    tile2 = lambda a: jnp.concatenate([a, a], axis=-1)
    cs_p = jnp.concatenate([cos_p, sin_p], axis=-1)
    cs_s = jnp.concatenate([cos_s, sin_s], axis=-1)

    xp = x_prompt
    xs = x_sample.reshape(ns, D_MODEL)
    conv_p, conv_s, v_rows = [], [], []
    khead_p = kcat_s = ckvt_p = ckv_p = kr_p = ckv_s = kr_s = None
    for layer in range(depth):
        if layer == n_a:
            kv_w = (row(kv_in_g), w_c, w_pe2, row(kv_g), wuk_all)
            ckv_p, kr_p, _, ckvt_p, khead_p = _kv_side(xp, cs_p, *kv_w, TQ)
            ckv_s, kr_s, kcat_s, _, _ = _kv_side(xs[None], cs_s, *kv_w, ns)
        pre_g, post_g = row(pre_mix_g[layer]), row(post_mix_g[layer])
        if layer < n_a:
            xp = _sgu_prompt(xp, pre_g, post_g, layer, w_in_b, row(sgu_g[layer]), w_s[layer],
                             bs_col[layer], w_out_b)
            xs, v = _sgu_sample(xs, pre_g, post_g, layer, w_in_b, row(sgu_g[layer]), row(wdiag[layer]),
                                row(bdiag[layer]), w_out_b)
            v_rows.append(v)
        else:
            j = layer - n_a
            qw = (w_dq_b[j], row(q_g[j]), wuq_n[j], wuq_pa[j], wuq_pb[j], wuk_t)
            xp = _mla_prompt(xp, khead_p, ckvt_p, cos_p.T, sin_p.T, pre_g, post_g, w_dq_b[j], row(q_g[j]),
                             wuqn_t[j], wuqa_t[j], wuqb_t[j], wuvt_h, wo_t[j])
            q = _q_sample(xs, tile2(cos_s), tile2(sin_s), pre_g, *qw)
            o = _decode(page_table, jnp.transpose(q, (1, 0, 2)), kcat_s.reshape(ns, 1, KCAT),
                        cache_ckv, cache_krt)
            xs = _out_sample(jnp.transpose(o, (1, 0, 2)), xs, post_g, wuv_h, w_o_b[j])
        pre_g, post_g = row(pre_ffn_g[layer]), row(post_ffn_g[layer])
        xp, tail = _ffn_prompt(xp, pre_g, post_g, layer, w_up_b, conv_w[layer], row(conv_b[layer]),
                               w_down_b)
        conv_p.append(tail)
        xs, st = _ffn_sample(xs, state_conv[layer], pre_g, post_g, layer, w_up_b, conv_w[layer],
                             row(conv_b[layer]), w_down_b)
        conv_s.append(st)

    return (xp, xs.reshape(ns, 1, D_MODEL), ckv_p, jnp.swapaxes(kr_p, 1, 2), jnp.stack(conv_p),
            ckv_s.reshape(ns, 1, KV_LORA), jnp.swapaxes(kr_s, 1, 2).reshape(ns, 1, QK_ROPE), jnp.stack(conv_s),
            jnp.stack(v_rows).reshape(n_a, ns, 1, SGU_WIDTH))
```

```python
import jax
import jax.numpy as jnp
from jax import lax
from jax.experimental import pallas as pl
from jax.experimental.pallas import tpu as pltpu

F32 = jnp.float32
BF16 = jnp.bfloat16

D_MODEL = 1024
N_A_LAYERS = 2
CHUNK = 128
SGU_WIDTH = 2 * D_MODEL
SGU_GROUPS = 8
SGU_GROUP_DIM = SGU_WIDTH // SGU_GROUPS
N_HEADS = 8
QK_NOPE = 128
QK_ROPE = 64
V_HEAD = 128
Q_LORA = D_MODEL // 2
KV_LORA = D_MODEL // 4
ROPE_THETA = 10000.0
ATTN_SCALE = (QK_NOPE + QK_ROPE) ** -0.5
D_FF = 11 * D_MODEL // 4
CONV_W = 3
EPS = 1e-6
PAGE_SIZE = 128

LANES = 128
BF16_ROWS = 16
VMEM_LIMIT = 56 * 1024 * 1024

KCAT = KV_LORA + LANES
KHEAD = QK_NOPE + LANES
FF_TILE = 256
N_FF = D_FF // FF_TILE
TM_FFN = 512
TM_SGU = 512
TQ = 256
DEC_CHUNK_PAGES = 8
DEC_SLOTS = 3


def _rms(x, g):
    return x * lax.rsqrt(jnp.mean(x * x, axis=-1, keepdims=True) + EPS) * g


def _gelu_erf(x):
    return 0.5 * x * (1.0 + lax.erf(x * (0.5 ** 0.5)))


def _dot(a, b):
    return jnp.dot(a, b, preferred_element_type=F32)


def _dot_nt(a, b):
    return lax.dot_general(a, b, (((1,), (1,)), ((), ())), preferred_element_type=F32)


def _params(n_grid_axes):
    return pltpu.CompilerParams(
        dimension_semantics=("arbitrary",) * n_grid_axes, vmem_limit_bytes=VMEM_LIMIT)


def _resident(shape):
    nd = len(shape)
    return pl.BlockSpec(shape, lambda *_: (0,) * nd, pipeline_mode=pl.Buffered(1))


def _layer_resident(stacked, layer):
    nd = stacked.ndim - 1
    return pl.BlockSpec((None,) + stacked.shape[1:], lambda *_: (layer,) + (0,) * nd,
                        pipeline_mode=pl.Buffered(1))


def _ffn_prompt_kernel(x_ref, pre_g_ref, post_g_ref, wup_ref, cw_ref, cb_ref, wdn_ref,
                       y_ref, tail_ref, act_ref, h_ref):
    tm = x_ref.shape[1]
    ng = tm // 8
    i = pl.program_id(1)

    @pl.when(i == 0)
    def _():
        tail_ref[...] = jnp.zeros(tail_ref.shape, F32)

    @pl.when(i >= 0)
    def _():
        hn = _rms(x_ref[0], pre_g_ref[...])
        h_ref[...] = jnp.swapaxes(hn.reshape(8, ng, hn.shape[-1]), 0, 1).reshape(tm, -1).astype(BF16)

    first = lax.broadcasted_iota(jnp.int32, (8, FF_TILE), 0) == 0

    def conv(col0):
        sl = slice(col0, col0 + FF_TILE)
        ap = _dot(h_ref[...], wup_ref[:, sl])
        prev = tail_ref[0, :, sl]
        sh1 = jnp.where(first, pltpu.roll(prev[8:], 1, 0), pltpu.roll(ap[tm - 8:], 1, 0))
        sh2 = jnp.where(first, pltpu.roll(prev[:8], 1, 0), pltpu.roll(ap[tm - 16:tm - 8], 1, 0))
        w = cw_ref[:, sl]
        c = cb_ref[:, sl] + jnp.concatenate([sh2, sh1, ap[:tm - 16]], axis=0) * w[0:1]
        c = c + jnp.concatenate([sh1, ap[:tm - 8]], axis=0) * w[1:2]
        c = c + ap * w[2:3]
        tail_ref[0, :, sl] = ap[tm - 16:]
        return c

    for f in range(N_FF):
        cg = conv(f * FF_TILE)
        cu = conv(D_FF + f * FF_TILE)
        act_ref[:, f * FF_TILE:(f + 1) * FF_TILE] = (jax.nn.gelu(cg, approximate=True) * cu).astype(BF16)
    fo = _rms(_dot(act_ref[...], wdn_ref[...]), post_g_ref[...])
    y_ref[0] = x_ref[0] + jnp.swapaxes(fo.reshape(ng, 8, fo.shape[-1]), 0, 1).reshape(tm, -1)


def _ffn_prompt(x, pre_g, post_g, layer, wup, cw, cb, wdn):
    b, t, d = x.shape
    tm = TM_FFN
    y, tail = pl.pallas_call(
        _ffn_prompt_kernel,
        grid=(b, t // tm),
        in_specs=[
            pl.BlockSpec((1, tm, d), lambda bi, i: (bi, i, 0)),
            _resident((1, d)), _resident((1, d)),
            _layer_resident(wup, layer), _resident(cw.shape), _resident(cb.shape),
            _layer_resident(wdn, layer),
        ],
        out_specs=[
            pl.BlockSpec((1, tm, d), lambda bi, i: (bi, i, 0)),
            pl.BlockSpec((1, 16, 2 * D_FF), lambda bi, i: (bi, 0, 0)),
        ],
        out_shape=[
            jax.ShapeDtypeStruct((b, t, d), F32),
            jax.ShapeDtypeStruct((b, 16, 2 * D_FF), F32),
        ],
        scratch_shapes=[pltpu.VMEM((tm, D_FF), BF16), pltpu.VMEM((tm, d), BF16)],
        compiler_params=_params(2),
        name="ffn_prompt",
    )(x, pre_g, post_g, wup, cw, cb, wdn)
    return y, tail[:, 7::8, :]


def _ffn_sample_kernel(x_ref, sg_ref, su_ref, pre_g_ref, post_g_ref,
                       wg_ref, wu_ref, cwg_ref, cwu_ref, cbg_ref, cbu_ref, wdn_ref,
                       y_ref, og_ref, ou_ref, h_ref, acc_ref):
    f = pl.program_id(0)

    @pl.when(f == 0)
    def _():
        h_ref[...] = _rms(x_ref[...], pre_g_ref[...]).astype(BF16)

    h = h_ref[...]

    def conv(w_ref, s_ref, cw_ref, cb_ref, o_ref):
        a = _dot(h, w_ref[...])
        p0, p1 = s_ref[:, 0, :], s_ref[:, 1, :]
        o_ref[:, 0, :] = p1
        o_ref[:, 1, :] = a
        w = cw_ref[...]
        c = cb_ref[...] + p0 * w[0:1]
        c = c + p1 * w[1:2]
        return c + a * w[2:3]

    cg = conv(wg_ref, sg_ref, cwg_ref, cbg_ref, og_ref)
    cu = conv(wu_ref, su_ref, cwu_ref, cbu_ref, ou_ref)
    act = (jax.nn.gelu(cg, approximate=True) * cu).astype(BF16)
    part = _dot(act, wdn_ref[...])

    @pl.when(f == 0)
    def _():
        acc_ref[...] = part

    @pl.when(f > 0)
    def _():
        acc_ref[...] += part

    @pl.when(f == pl.num_programs(0) - 1)
    def _():
        y_ref[...] = x_ref[...] + _rms(acc_ref[...], post_g_ref[...])


def _ffn_sample(x, state_all, pre_g, post_g, layer, wup, cw, cb, wdn):
    n, d = x.shape
    tf = FF_TILE
    hrows = CONV_W - 1
    state = lambda off: pl.BlockSpec((None, n, hrows, tf), lambda f: (layer, 0, 0, off + f))
    row3 = lambda off: pl.BlockSpec((CONV_W, tf), lambda f: (0, off + f))
    row1 = lambda off: pl.BlockSpec((1, tf), lambda f: (0, off + f))
    const = lambda shape: pl.BlockSpec(shape, lambda f: (0,) * len(shape))
    half = pl.BlockSpec((n, hrows, tf), lambda f: (0, 0, f))
    y, og, ou = pl.pallas_call(
        _ffn_sample_kernel,
        grid=(N_FF,),
        in_specs=[
            const((n, d)),
            state(0), state(N_FF),
            const((1, d)), const((1, d)),
            pl.BlockSpec((None, d, tf), lambda f: (layer, 0, f)),
            pl.BlockSpec((None, d, tf), lambda f: (layer, 0, N_FF + f)),
            row3(0), row3(N_FF), row1(0), row1(N_FF),
            pl.BlockSpec((None, tf, d), lambda f: (layer, f, 0)),
        ],
        out_specs=[const((n, d)), half, half],
        out_shape=[
            jax.ShapeDtypeStruct((n, d), F32),
            jax.ShapeDtypeStruct((n, hrows, D_FF), F32),
            jax.ShapeDtypeStruct((n, hrows, D_FF), F32),
        ],
        scratch_shapes=[pltpu.VMEM((n, d), BF16), pltpu.VMEM((n, d), F32)],
        compiler_params=_params(1),
        name="ffn_sample",
    )(x, state_all, state_all, pre_g, post_g, wup, wup, cw, cw, cb, cb, wdn)
    return y, jnp.concatenate([og, ou], axis=-1)


def _sgu_prompt_kernel(x_ref, pre_g_ref, post_g_ref, win_ref, sgug_ref, ws_ref, bs_ref, wout_ref,
                       y_ref, v_ref, gated_ref, h_ref):
    tm = x_ref.shape[1]
    gd = SGU_GROUP_DIM
    x = x_ref[0]

    @pl.when(pl.program_id(1) >= 0)
    def _():
        h_ref[...] = _rms(x, pre_g_ref[...]).astype(BF16)

    ss = jnp.zeros((tm, 1), F32)
    for j in range(SGU_GROUPS):
        zc = _gelu_erf(_dot(h_ref[...], win_ref[:, SGU_WIDTH + j * gd:SGU_WIDTH + (j + 1) * gd]))
        v_ref[:, j * gd:(j + 1) * gd] = zc
        ss = ss + jnp.sum(zc * zc, axis=-1, keepdims=True)
    inv = lax.rsqrt(ss * (1.0 / SGU_WIDTH) + EPS)
    causal = (lax.broadcasted_iota(jnp.int32, (CHUNK, CHUNK), 0)
              >= lax.broadcasted_iota(jnp.int32, (CHUNK, CHUNK), 1))
    for g in range(SGU_GROUPS):
        sl = slice(g * gd, (g + 1) * gd)
        vn = (v_ref[:, sl] * inv * sgug_ref[:, sl]).astype(BF16)
        wm = jnp.where(causal, ws_ref[g], 0.0).astype(BF16)
        bias = bs_ref[g]
        s = jnp.concatenate(
            [_dot(wm, vn[c * CHUNK:(c + 1) * CHUNK]) + bias for c in range(tm // CHUNK)], axis=0)
        u = _gelu_erf(_dot(h_ref[...], win_ref[:, sl]))
        gated_ref[:, sl] = (u * s).astype(BF16)
    y_ref[0] = x + _rms(_dot(gated_ref[...], wout_ref[...]), post_g_ref[...])


def _sgu_prompt(x, pre_g, post_g, layer, win, sgug, ws, bs, wout):
    b, t, d = x.shape
    tm = TM_SGU
    return pl.pallas_call(
        _sgu_prompt_kernel,
        grid=(b, t // tm),
        in_specs=[
            pl.BlockSpec((1, tm, d), lambda bi, i: (bi, i, 0)),
            _resident((1, d)), _resident((1, d)),
            _layer_resident(win, layer), _resident(sgug.shape), _resident(ws.shape), _resident(bs.shape),
            _layer_resident(wout, layer),
        ],
        out_specs=pl.BlockSpec((1, tm, d), lambda bi, i: (bi, i, 0)),
        out_shape=jax.ShapeDtypeStruct((b, t, d), F32),
        scratch_shapes=[pltpu.VMEM((tm, SGU_WIDTH), F32), pltpu.VMEM((tm, SGU_WIDTH), BF16),
                        pltpu.VMEM((tm, d), BF16)],
        compiler_params=_params(2),
        name="sgu_prompt",
    )(x, pre_g, post_g, win, sgug, ws, bs, wout)


def _sgu_sample_kernel(x_ref, pre_g_ref, post_g_ref, win_ref, sgug_ref, wdiag_ref, bdiag_ref, wout_ref,
                       y_ref, v_ref):
    x = x_ref[...]
    h = _rms(x, pre_g_ref[...]).astype(BF16)
    u = _gelu_erf(_dot(h, win_ref[:, :SGU_WIDTH]))
    v = _gelu_erf(_dot(h, win_ref[:, SGU_WIDTH:]))
    vn = _rms(v, sgug_ref[...])
    v_ref[...] = vn
    s = vn * wdiag_ref[...] + bdiag_ref[...]
    y = _dot((u * s).astype(BF16), wout_ref[...])
    y_ref[...] = x + _rms(y, post_g_ref[...])


def _sgu_sample(x, pre_g, post_g, layer, win, sgug, wdiag, bdiag, wout):
    n, d = x.shape
    whole = lambda shape: pl.BlockSpec(shape, lambda i: (0,) * len(shape))
    return pl.pallas_call(
        _sgu_sample_kernel,
        grid=(1,),
        in_specs=[whole((n, d)), whole((1, d)), whole((1, d)), _layer_resident(win, layer),
                  whole((1, SGU_WIDTH)), whole((1, SGU_WIDTH)), whole((1, SGU_WIDTH)),
                  _layer_resident(wout, layer)],
        out_specs=[whole((n, d)), whole((n, SGU_WIDTH))],
        out_shape=[jax.ShapeDtypeStruct((n, d), F32), jax.ShapeDtypeStruct((n, SGU_WIDTH), F32)],
        compiler_params=_params(1),
        name="sgu_sample",
    )(x, pre_g, post_g, win, sgug, wdiag, bdiag, wout)


def _kv_side_kernel(x_ref, cs_ref, g_in_ref, wc_ref, wpe_ref, kvg_ref, wuk_ref,
                    ckv_ref, kr_ref, kcat_ref, ckvt_ref, khead_ref):
    h = _rms(x_ref[0], g_in_ref[...]).astype(BF16)
    ckv = _rms(_dot(h, wc_ref[...]), kvg_ref[...])
    t = _dot(h, wpe_ref[...]) * cs_ref[...]
    t = t + pltpu.roll(t, QK_ROPE, 1)
    lane = lax.broadcasted_iota(jnp.int32, t.shape, 1)
    krp = jnp.where(lane < QK_ROPE, t, 0.0)
    ckv_ref[0] = ckv
    kr_ref[0] = t.T[:QK_ROPE]
    kcat_ref[0] = jnp.concatenate([ckv, krp], axis=1).astype(BF16)
    ckvt_ref[0, 0] = ckv.T.astype(BF16)
    kn = _dot(ckv.astype(BF16), wuk_ref[...])
    for hd in range(N_HEADS):
        khead_ref[0, hd] = jnp.concatenate(
            [kn[:, hd * QK_NOPE:(hd + 1) * QK_NOPE], krp], axis=1).astype(BF16)


def _kv_side(x, cs_tab, g_in, wc, wpe, kvg, wuk, tm):
    b, t, d = x.shape
    return pl.pallas_call(
        _kv_side_kernel,
        grid=(b, t // tm),
        in_specs=[
            pl.BlockSpec((1, tm, d), lambda bi, i: (bi, i, 0)),
            pl.BlockSpec((tm, LANES), lambda bi, i: (i, 0)),
            _resident((1, d)), _resident(wc.shape), _resident(wpe.shape), _resident((1, KV_LORA)),
            _resident(wuk.shape),
        ],
        out_specs=[
            pl.BlockSpec((1, tm, KV_LORA), lambda bi, i: (bi, i, 0)),
            pl.BlockSpec((1, QK_ROPE, tm), lambda bi, i: (bi, 0, i)),
            pl.BlockSpec((1, tm, KCAT), lambda bi, i: (bi, i, 0)),
            pl.BlockSpec((1, 1, KV_LORA, tm), lambda bi, i: (bi, i, 0, 0)),
            pl.BlockSpec((1, N_HEADS, tm, KHEAD), lambda bi, i: (bi, 0, i, 0)),
        ],
        out_shape=[
            jax.ShapeDtypeStruct((b, t, KV_LORA), F32),
            jax.ShapeDtypeStruct((b, QK_ROPE, t), F32),
            jax.ShapeDtypeStruct((b, t, KCAT), BF16),
            jax.ShapeDtypeStruct((b, t // tm, KV_LORA, tm), BF16),
            jax.ShapeDtypeStruct((b, N_HEADS, t, KHEAD), BF16),
        ],
        compiler_params=_params(2),
        name="kv_side",
    )(x, cs_tab, g_in, wc, wpe, kvg, wuk)


def _q_rows(x, cos, sin, pre_g, wdq, qg, wuq_n, wuq_pa, wuq_pb, wuk_t, store):
    h = _rms(x, pre_g).astype(BF16)
    cq = _rms(_dot(h, wdq[...]), qg).astype(BF16)
    qn = _dot(cq, wuq_n[...])
    qa = _dot(cq, wuq_pa[...])
    qb = _dot(cq, wuq_pb[...])
    for hd in range(N_HEADS):
        sl = slice(hd * LANES, (hd + 1) * LANES)
        qpe = qa[:, sl] * cos + qb[:, sl] * sin
        ql = _dot(qn[:, sl].astype(BF16), wuk_t[hd])
        store(hd, (ql * ATTN_SCALE).astype(BF16), (qpe * ATTN_SCALE).astype(BF16))


def _mla_out(o_heads, x, post_g, wuv, wo, o_ref):
    for hd in range(N_HEADS):
        o_ref[:, hd * V_HEAD:(hd + 1) * V_HEAD] = _dot(o_heads(hd).astype(BF16), wuv[hd]).astype(BF16)
    m = _dot(o_ref[...], wo[...])
    return x + _rms(m, post_g)


def _mla_prompt_kernel(x_ref, k_ref, vt_ref, cos_ref, sin_ref, pre_g_ref, post_g_ref, wdq_ref, qg_ref,
                       wuqn_ref, wuqa_ref, wuqb_ref, wuvt_ref, wot_ref,
                       y_ref, qt_ref, m_ref, l_ref, acc_ref, o_ref):
    tq = x_ref.shape[1]
    qi = pl.program_id(1)
    x = x_ref[0]

    h = _rms(x, pre_g_ref[...]).astype(BF16)
    cq = _rms(_dot(h, wdq_ref[...]), qg_ref[...])
    cq_t = cq.T.astype(BF16)
    qn_t = _dot(wuqn_ref[...], cq_t)
    qa_t = _dot(wuqa_ref[...], cq_t)
    qb_t = _dot(wuqb_ref[...], cq_t)
    cos_t, sin_t = cos_ref[...], sin_ref[...]
    for hd in range(N_HEADS):
        cols = slice(hd * tq, (hd + 1) * tq)
        rows = slice(hd * QK_ROPE, (hd + 1) * QK_ROPE)
        qpe_t = qa_t[rows] * cos_t + qb_t[rows] * sin_t
        qt_ref[:QK_NOPE, cols] = (qn_t[hd * QK_NOPE:(hd + 1) * QK_NOPE] * ATTN_SCALE).astype(BF16)
        qt_ref[QK_NOPE:QK_NOPE + QK_ROPE, cols] = (qpe_t * ATTN_SCALE).astype(BF16)
    qt_ref[QK_NOPE + QK_ROPE:, :] = jnp.zeros((KHEAD - QK_NOPE - QK_ROPE, N_HEADS * tq), BF16)

    def step(kb, diagonal):
        keys = pl.ds(pl.multiple_of(kb * tq, tq), tq)
        vt = vt_ref[0, kb]
        scores = [_dot(k_ref[0, hd, keys, :], qt_ref[:, hd * tq:(hd + 1) * tq])
                  for hd in range(N_HEADS)]
        for hd in range(N_HEADS):
            cols = slice(hd * tq, (hd + 1) * tq)
            st = scores[hd]
            if diagonal:
                key = lax.broadcasted_iota(jnp.int32, st.shape, 0)
                qry = lax.broadcasted_iota(jnp.int32, st.shape, 1)
                st = jnp.where(key <= qry, st, -jnp.inf)
                m_new = jnp.max(st, axis=0, keepdims=True)
                p = jnp.exp(st - m_new)
                l_ref[hd:hd + 1, :] = jnp.sum(p, axis=0, keepdims=True)
                acc_ref[:, cols] = _dot(vt, p.astype(BF16))
            else:
                m_old = m_ref[hd:hd + 1, :]
                m_new = jnp.maximum(m_old, jnp.max(st, axis=0, keepdims=True))
                alpha = jnp.exp(m_old - m_new)
                p = jnp.exp(st - m_new)
                l_ref[hd:hd + 1, :] = l_ref[hd:hd + 1, :] * alpha + jnp.sum(p, axis=0, keepdims=True)
                acc_ref[:, cols] = acc_ref[:, cols] * alpha + _dot(vt, p.astype(BF16))
            m_ref[hd:hd + 1, :] = m_new

    def body(kb, carry):
        step(kb, False)
        return carry

    step(qi, True)
    lax.fori_loop(0, qi, body, 0)

    for hd in range(N_HEADS):
        cols = slice(hd * tq, (hd + 1) * tq)
        o_t = acc_ref[:, cols] * (1.0 / l_ref[hd:hd + 1, :])
        o_ref[hd * V_HEAD:(hd + 1) * V_HEAD, :] = _dot(wuvt_ref[hd], o_t.astype(BF16)).astype(BF16)
    m_t = _dot(wot_ref[...], o_ref[...])
    y_ref[0] = x + _rms(m_t.T, post_g_ref[...])


def _mla_prompt(x, khead, ckv_t, cos_t, sin_t, pre_g, post_g, wdq, qg, wuqn_t, wuqa_t, wuqb_t,
                wuvt_h, wo_t):
    b, t, d = x.shape
    tq = TQ
    lanes = N_HEADS * tq
    weights = (wdq, qg, wuqn_t, wuqa_t, wuqb_t, wuvt_h, wo_t)
    return pl.pallas_call(
        _mla_prompt_kernel,
        grid=(b, t // tq),
        in_specs=[
            pl.BlockSpec((1, tq, d), lambda bi, i: (bi, i, 0)),
            pl.BlockSpec((1, N_HEADS, t, KHEAD), lambda bi, i: (bi, 0, 0, 0)),
            pl.BlockSpec((1, t // tq, KV_LORA, tq), lambda bi, i: (bi, 0, 0, 0)),
            pl.BlockSpec((QK_ROPE, tq), lambda bi, i: (0, i)),
            pl.BlockSpec((QK_ROPE, tq), lambda bi, i: (0, i)),
            _resident((1, d)), _resident((1, d)),
        ] + [_resident(w.shape) for w in weights],
        out_specs=pl.BlockSpec((1, tq, d), lambda bi, i: (bi, i, 0)),
        out_shape=jax.ShapeDtypeStruct((b, t, d), F32),
        scratch_shapes=[
            pltpu.VMEM((KHEAD, lanes), BF16),
            pltpu.VMEM((N_HEADS, tq), F32), pltpu.VMEM((N_HEADS, tq), F32),
            pltpu.VMEM((KV_LORA, lanes), F32),
            pltpu.VMEM((N_HEADS * V_HEAD, tq), BF16),
        ],
        compiler_params=_params(2),
        name="mla_prompt",
    )(x, khead, ckv_t, cos_t, sin_t, pre_g, post_g, *weights)


def _q_sample_kernel(x_ref, cos_ref, sin_ref, pre_g_ref, wdq_ref, qg_ref, wuqn_ref, wuqa_ref, wuqb_ref,
                     wukt_ref, q_ref):
    def store(hd, ql, qpe):
        q_ref[hd, :, :KV_LORA] = ql
        q_ref[hd, :, KV_LORA:] = qpe

    _q_rows(x_ref[...], cos_ref[...], sin_ref[...], pre_g_ref[...], wdq_ref, qg_ref[...],
            wuqn_ref, wuqa_ref, wuqb_ref, wukt_ref, store)


def _q_sample(x, cos, sin, pre_g, wdq, qg, wuq_n, wuq_pa, wuq_pb, wuk_t):
    n = x.shape[0]
    return pl.pallas_call(
        _q_sample_kernel,
        out_shape=jax.ShapeDtypeStruct((N_HEADS, n, KCAT), BF16),
        compiler_params=pltpu.CompilerParams(vmem_limit_bytes=VMEM_LIMIT),
        name="q_sample",
    )(x, cos, sin, pre_g, wdq, qg, wuq_n, wuq_pa, wuq_pb, wuk_t)


def _decode_kernel(pt_ref, q_ref, knew_ref, ckv_hbm, krt_hbm, o_ref, cbuf, rbuf, sem):
    b = pl.program_id(0)
    nb = pl.num_programs(0)
    n_pages = cbuf.shape[1]
    n_chunks = n_pages // DEC_CHUNK_PAGES
    cp_rows = DEC_CHUNK_PAGES * PAGE_SIZE
    ahead = DEC_SLOTS - 1

    def page_copies(page, slot, p):
        return (pltpu.make_async_copy(ckv_hbm.at[page], cbuf.at[slot, p], sem.at[0, slot]),
                pltpu.make_async_copy(krt_hbm.at[page], rbuf.at[slot, p], sem.at[1, slot]))

    def start_pages(seq, slot, p_lo, p_hi):
        for p in range(p_lo, p_hi):
            for cp in page_copies(pt_ref[seq, p], slot, p):
                cp.start(priority=p % 2)

    def wait_slot(slot):
        for p in range(n_pages):
            for cp in page_copies(0, slot, p):
                cp.wait()

    slot = b % DEC_SLOTS
    nxt = jnp.minimum(b + ahead, nb - 1)
    nxt_slot = (b + ahead) % DEC_SLOTS

    @pl.when(b == 0)
    def _():
        for s in range(ahead):
            start_pages(s, s, 0, n_pages)

    wait_slot(slot)

    q = q_ref[0]
    ql = q[:, :KV_LORA]
    qp = q[:, KV_LORA:KV_LORA + QK_ROPE]

    def chunk_scores(c):
        start_pages(nxt, nxt_slot, c * DEC_CHUNK_PAGES, (c + 1) * DEC_CHUNK_PAGES)
        s_pe = jnp.concatenate(
            [_dot(qp, rbuf[slot, c * DEC_CHUNK_PAGES + j].astype(BF16)) for j in range(DEC_CHUNK_PAGES)],
            axis=1)
        rows = cbuf[slot, c * DEC_CHUNK_PAGES:(c + 1) * DEC_CHUNK_PAGES].reshape(cp_rows, KV_LORA)
        return _dot(ql, rows.T.astype(BF16)) + s_pe, rows.astype(BF16)

    kn = knew_ref[0].astype(F32)
    m = jnp.sum(q.astype(F32) * kn, axis=-1, keepdims=True)
    l = jnp.ones_like(m)
    acc = jnp.broadcast_to(kn[:, :KV_LORA], (N_HEADS, KV_LORA))
    nxt_chunk = chunk_scores(0)
    for c in range(n_chunks):
        s, kc = nxt_chunk
        if c + 1 < n_chunks:
            nxt_chunk = chunk_scores(c + 1)
        m_new = jnp.maximum(m, jnp.max(s, axis=-1, keepdims=True))
        alpha = jnp.exp(m - m_new)
        p = jnp.exp(s - m_new)
        l = l * alpha + jnp.sum(p, axis=-1, keepdims=True)
        acc = acc * alpha + _dot(p.astype(BF16), kc)
        m = m_new
    o_ref[0] = acc / l

    @pl.when(b == nb - 1)
    def _():
        for s in range(1, DEC_SLOTS):
            wait_slot((slot + s) % DEC_SLOTS)


def _decode(page_table, q, knew, cache_ckv, cache_krt):
    n, n_pages = page_table.shape
    grid_spec = pltpu.PrefetchScalarGridSpec(
        num_scalar_prefetch=1,
        grid=(n,),
        in_specs=[
            pl.BlockSpec((1, N_HEADS, KCAT), lambda bi, pt: (bi, 0, 0)),
            pl.BlockSpec((1, 1, KCAT), lambda bi, pt: (bi, 0, 0)),
            pl.BlockSpec(memory_space=pl.ANY),
            pl.BlockSpec(memory_space=pl.ANY),
        ],
        out_specs=pl.BlockSpec((1, N_HEADS, KV_LORA), lambda bi, pt: (bi, 0, 0)),
        scratch_shapes=[
            pltpu.VMEM((DEC_SLOTS, n_pages, PAGE_SIZE, KV_LORA), F32),
            pltpu.VMEM((DEC_SLOTS, n_pages, QK_ROPE, PAGE_SIZE), F32),
            pltpu.SemaphoreType.DMA((2, DEC_SLOTS)),
        ],
    )
    return pl.pallas_call(
        _decode_kernel,
        grid_spec=grid_spec,
        out_shape=jax.ShapeDtypeStruct((n, N_HEADS, KV_LORA), F32),
        compiler_params=_params(1),
        name="mla_decode",
    )(page_table, q, knew, cache_ckv, cache_krt)


def _out_sample_kernel(o_ref, x_ref, post_g_ref, wuv_ref, wo_ref, y_ref, os_ref):
    y_ref[...] = _mla_out(lambda hd: o_ref[hd], x_ref[...], post_g_ref[...], wuv_ref, wo_ref, os_ref)


def _out_sample(o_heads, x, post_g, wuv, wo):
    n, d = x.shape
    return pl.pallas_call(
        _out_sample_kernel,
        out_shape=jax.ShapeDtypeStruct((n, d), F32),
        scratch_shapes=[pltpu.VMEM((n, N_HEADS * V_HEAD), BF16)],
        compiler_params=pltpu.CompilerParams(vmem_limit_bytes=VMEM_LIMIT),
        name="out_sample",
    )(o_heads, x, post_g, wuv, wo)


def _rot_half_cols(w):
    half = QK_ROPE // 2
    return jnp.concatenate([-w[..., half:], w[..., :half]], axis=-1)


def _rope_tables(pos):
    half = QK_ROPE // 2
    inv = 1.0 / (ROPE_THETA ** (jnp.arange(half, dtype=F32) / half))
    ang = pos.astype(F32)[:, None] * inv[None, :]
    cos, sin = jnp.cos(ang), jnp.sin(ang)
    cos64 = jnp.concatenate([cos, cos], axis=-1)
    sin64 = jnp.concatenate([sin, sin], axis=-1)
    return cos64, sin64


def kernel(x_prompt, x_sample, cache_ckv, cache_kr, state_conv, page_table, pre_mix_g, post_mix_g,
           pre_ffn_g, post_ffn_g, w_in_a, sgu_g, w_s, b_s, w_out_a, kv_in_g, w_dkv, kv_g, w_uk, w_uv,
           w_dq, q_g, w_uq, w_o, w_up, conv_w, conv_b, w_down):
    depth = w_up.shape[0]
    n_a = w_in_a.shape[0]
    bp, tp, _ = x_prompt.shape
    ns = x_sample.shape[0]
    past_len = page_table.shape[1] * PAGE_SIZE
    row = lambda v: v.reshape(1, -1)

    w_in_b, w_out_b = w_in_a.astype(BF16), w_out_a.astype(BF16)
    w_up_b, w_down_b = w_up.astype(BF16), w_down.astype(BF16)
    w_dq_b, w_o_b = w_dq.astype(BF16), w_o.astype(BF16)
    wuq = w_uq.reshape(-1, Q_LORA, N_HEADS, QK_NOPE + QK_ROPE)
    wuq_n = wuq[..., :QK_NOPE].reshape(-1, Q_LORA, N_HEADS * QK_NOPE).astype(BF16)
    pad = lambda w: jnp.pad(w, ((0, 0),) * 3 + ((0, LANES - QK_ROPE),)).reshape(
        -1, Q_LORA, N_HEADS * LANES).astype(BF16)
    wuq_pa = pad(wuq[..., QK_NOPE:])
    wuq_pb = pad(_rot_half_cols(wuq[..., QK_NOPE:]))
    wuk_t = jnp.transpose(w_uk, (1, 2, 0)).astype(BF16)
    wuv_h = jnp.transpose(w_uv, (1, 0, 2)).astype(BF16)
    wuqn_t = jnp.swapaxes(wuq_n, 1, 2)
    flat_t = lambda w: jnp.swapaxes(w.reshape(-1, Q_LORA, N_HEADS * QK_ROPE), 1, 2).astype(BF16)
    wuqa_t = flat_t(wuq[..., QK_NOPE:])
    wuqb_t = flat_t(_rot_half_cols(wuq[..., QK_NOPE:]))
    wuk_all = w_uk.reshape(KV_LORA, N_HEADS * QK_NOPE).astype(BF16)
    wuvt_h = jnp.transpose(w_uv, (1, 2, 0)).astype(BF16)
    wo_t = jnp.swapaxes(w_o_b, 1, 2)
    cache_krt = jnp.swapaxes(cache_kr, 1, 2)
    w_c = w_dkv[:, :KV_LORA].astype(BF16)
    w_pe = w_dkv[:, KV_LORA:]
    w_pe2 = jnp.concatenate([w_pe, _rot_half_cols(w_pe)], axis=-1).astype(BF16)
    bs_col = b_s[..., None]
    wdiag = jnp.repeat(w_s[:, :, 0, 0], SGU_GROUP_DIM, axis=-1)
    bdiag = jnp.repeat(b_s[:, :, 0], SGU_GROUP_DIM, axis=-1)

    cos_p, sin_p = _rope_tables(jnp.arange(tp, dtype=jnp.int32))
    cos_s, sin_s = _rope_tables(jnp.full((ns,), past_len, dtype=jnp.int32))
    tile2 = lambda a: jnp.concatenate([a, a], axis=-1)
    cs_p = jnp.concatenate([cos_p, sin_p], axis=-1)
    cs_s = jnp.concatenate([cos_s, sin_s], axis=-1)

    xp = x_prompt
    xs = x_sample.reshape(ns, D_MODEL)
    conv_p, conv_s, v_rows = [], [], []
    khead_p = kcat_s = ckvt_p = ckv_p = kr_p = ckv_s = kr_s = None
    for layer in range(depth):
        if layer == n_a:
            kv_w = (row(kv_in_g), w_c, w_pe2, row(kv_g), wuk_all)
            ckv_p, kr_p, _, ckvt_p, khead_p = _kv_side(xp, cs_p, *kv_w, TQ)
            ckv_s, kr_s, kcat_s, _, _ = _kv_side(xs[None], cs_s, *kv_w, ns)
        pre_g, post_g = row(pre_mix_g[layer]), row(post_mix_g[layer])
        if layer < n_a:
            xp = _sgu_prompt(xp, pre_g, post_g, layer, w_in_b, row(sgu_g[layer]), w_s[layer],
                             bs_col[layer], w_out_b)
            xs, v = _sgu_sample(xs, pre_g, post_g, layer, w_in_b, row(sgu_g[layer]), row(wdiag[layer]),
                                row(bdiag[layer]), w_out_b)
            v_rows.append(v)
        else:
            j = layer - n_a
            qw = (w_dq_b[j], row(q_g[j]), wuq_n[j], wuq_pa[j], wuq_pb[j], wuk_t)
            xp = _mla_prompt(xp, khead_p, ckvt_p, cos_p.T, sin_p.T, pre_g, post_g, w_dq_b[j], row(q_g[j]),
                             wuqn_t[j], wuqa_t[j], wuqb_t[j], wuvt_h, wo_t[j])
            q = _q_sample(xs, tile2(cos_s), tile2(sin_s), pre_g, *qw)
            o = _decode(page_table, jnp.transpose(q, (1, 0, 2)), kcat_s.reshape(ns, 1, KCAT),
                        cache_ckv, cache_krt)
            xs = _out_sample(jnp.transpose(o, (1, 0, 2)), xs, post_g, wuv_h, w_o_b[j])
        pre_g, post_g = row(pre_ffn_g[layer]), row(post_ffn_g[layer])
        xp, tail = _ffn_prompt(xp, pre_g, post_g, layer, w_up_b, conv_w[layer], row(conv_b[layer]),
                               w_down_b)
        conv_p.append(tail)
        xs, st = _ffn_sample(xs, state_conv, pre_g, post_g, layer, w_up_b, conv_w[layer],
                             row(conv_b[layer]), w_down_b)
        conv_s.append(st)

    return (xp, xs.reshape(ns, 1, D_MODEL), ckv_p, jnp.swapaxes(kr_p, 1, 2), jnp.stack(conv_p),
            ckv_s.reshape(ns, 1, KV_LORA), jnp.swapaxes(kr_s, 1, 2).reshape(ns, 1, QK_ROPE), jnp.stack(conv_s),
            jnp.stack(v_rows).reshape(n_a, ns, 1, SGU_WIDTH))
```

```python
import jax
import jax.numpy as jnp
from jax import lax
from jax.experimental import pallas as pl
from jax.experimental.pallas import tpu as pltpu

F32 = jnp.float32
BF16 = jnp.bfloat16

D_MODEL = 1024
N_A_LAYERS = 2
CHUNK = 128
SGU_WIDTH = 2 * D_MODEL
SGU_GROUPS = 8
SGU_GROUP_DIM = SGU_WIDTH // SGU_GROUPS
N_HEADS = 8
QK_NOPE = 128
QK_ROPE = 64
V_HEAD = 128
Q_LORA = D_MODEL // 2
KV_LORA = D_MODEL // 4
ROPE_THETA = 10000.0
ATTN_SCALE = (QK_NOPE + QK_ROPE) ** -0.5
D_FF = 11 * D_MODEL // 4
CONV_W = 3
EPS = 1e-6
PAGE_SIZE = 128

LANES = 128
BF16_ROWS = 16
VMEM_LIMIT = 56 * 1024 * 1024

KCAT = KV_LORA + LANES
KHEAD = QK_NOPE + LANES
VROWS = KV_LORA + BF16_ROWS
FF_TILE = 256
N_FF = D_FF // FF_TILE
TM_FFN = 512
TM_SGU = 512
TQ = 256
TM_KV = 512
DEC_CHUNK_PAGES = 8
DEC_SLOTS = 3


def _rms(x, g):
    return x * lax.rsqrt(jnp.mean(x * x, axis=-1, keepdims=True) + EPS) * g


def _gelu_erf(x):
    return 0.5 * x * (1.0 + lax.erf(x * (0.5 ** 0.5)))


def _dot(a, b):
    return jnp.dot(a, b, preferred_element_type=F32)


def _dot_nt(a, b):
    return lax.dot_general(a, b, (((1,), (1,)), ((), ())), preferred_element_type=F32)


def _params(n_grid_axes):
    return pltpu.CompilerParams(
        dimension_semantics=("arbitrary",) * n_grid_axes, vmem_limit_bytes=VMEM_LIMIT)


def _resident(shape):
    nd = len(shape)
    return pl.BlockSpec(shape, lambda *_: (0,) * nd, pipeline_mode=pl.Buffered(1))


def _layer_resident(stacked, layer):
    nd = stacked.ndim - 1
    return pl.BlockSpec((None,) + stacked.shape[1:], lambda *_: (layer,) + (0,) * nd,
                        pipeline_mode=pl.Buffered(1))


def _ffn_prompt_kernel(x_ref, pre_g_ref, post_g_ref, wup_ref, cw_ref, cb_ref, wdn_ref,
                       y_ref, tail_ref, act_ref, h_ref):
    tm = x_ref.shape[1]
    ng = tm // 8
    i = pl.program_id(1)

    @pl.when(i == 0)
    def _():
        tail_ref[...] = jnp.zeros(tail_ref.shape, F32)

    @pl.when(i >= 0)
    def _():
        hn = _rms(x_ref[0], pre_g_ref[...])
        h_ref[...] = jnp.swapaxes(hn.reshape(8, ng, hn.shape[-1]), 0, 1).reshape(tm, -1).astype(BF16)

    first = lax.broadcasted_iota(jnp.int32, (8, FF_TILE), 0) == 0

    def conv(col0):
        sl = slice(col0, col0 + FF_TILE)
        ap = _dot(h_ref[...], wup_ref[:, sl])
        prev = tail_ref[0, :, sl]
        sh1 = jnp.where(first, pltpu.roll(prev[8:], 1, 0), pltpu.roll(ap[tm - 8:], 1, 0))
        sh2 = jnp.where(first, pltpu.roll(prev[:8], 1, 0), pltpu.roll(ap[tm - 16:tm - 8], 1, 0))
        w = cw_ref[:, sl]
        c = cb_ref[:, sl] + jnp.concatenate([sh2, sh1, ap[:tm - 16]], axis=0) * w[0:1]
        c = c + jnp.concatenate([sh1, ap[:tm - 8]], axis=0) * w[1:2]
        c = c + ap * w[2:3]
        tail_ref[0, :, sl] = ap[tm - 16:]
        return c

    for f in range(N_FF):
        cg = conv(f * FF_TILE)
        cu = conv(D_FF + f * FF_TILE)
        act_ref[:, f * FF_TILE:(f + 1) * FF_TILE] = (jax.nn.gelu(cg, approximate=True) * cu).astype(BF16)
    fo = _rms(_dot(act_ref[...], wdn_ref[...]), post_g_ref[...])
    y_ref[0] = x_ref[0] + jnp.swapaxes(fo.reshape(ng, 8, fo.shape[-1]), 0, 1).reshape(tm, -1)


def _ffn_prompt(x, pre_g, post_g, layer, wup, cw, cb, wdn):
    b, t, d = x.shape
    tm = TM_FFN
    y, tail = pl.pallas_call(
        _ffn_prompt_kernel,
        grid=(b, t // tm),
        in_specs=[
            pl.BlockSpec((1, tm, d), lambda bi, i: (bi, i, 0)),
            _resident((1, d)), _resident((1, d)),
            _layer_resident(wup, layer), _resident(cw.shape), _resident(cb.shape),
            _layer_resident(wdn, layer),
        ],
        out_specs=[
            pl.BlockSpec((1, tm, d), lambda bi, i: (bi, i, 0)),
            pl.BlockSpec((1, 16, 2 * D_FF), lambda bi, i: (bi, 0, 0)),
        ],
        out_shape=[
            jax.ShapeDtypeStruct((b, t, d), F32),
            jax.ShapeDtypeStruct((b, 16, 2 * D_FF), F32),
        ],
        scratch_shapes=[pltpu.VMEM((tm, D_FF), BF16), pltpu.VMEM((tm, d), BF16)],
        compiler_params=_params(2),
        name="ffn_prompt",
    )(x, pre_g, post_g, wup, cw, cb, wdn)
    return y, tail[:, 7::8, :]


def _ffn_sample_kernel(x_ref, sg_ref, su_ref, pre_g_ref, post_g_ref,
                       wg_ref, wu_ref, cwg_ref, cwu_ref, cbg_ref, cbu_ref, wdn_ref,
                       y_ref, og_ref, ou_ref, h_ref, acc_ref):
    f = pl.program_id(0)

    @pl.when(f == 0)
    def _():
        h_ref[...] = _rms(x_ref[...], pre_g_ref[...]).astype(BF16)

    h = h_ref[...]

    def conv(w_ref, s_ref, cw_ref, cb_ref, o_ref):
        a = _dot(h, w_ref[...])
        p0, p1 = s_ref[:, 0, :], s_ref[:, 1, :]
        o_ref[:, 0, :] = p1
        o_ref[:, 1, :] = a
        w = cw_ref[...]
        c = cb_ref[...] + p0 * w[0:1]
        c = c + p1 * w[1:2]
        return c + a * w[2:3]

    cg = conv(wg_ref, sg_ref, cwg_ref, cbg_ref, og_ref)
    cu = conv(wu_ref, su_ref, cwu_ref, cbu_ref, ou_ref)
    act = (jax.nn.gelu(cg, approximate=True) * cu).astype(BF16)
    part = _dot(act, wdn_ref[...])

    @pl.when(f == 0)
    def _():
        acc_ref[...] = part

    @pl.when(f > 0)
    def _():
        acc_ref[...] += part

    @pl.when(f == pl.num_programs(0) - 1)
    def _():
        y_ref[...] = x_ref[...] + _rms(acc_ref[...], post_g_ref[...])


def _ffn_sample(x, state_all, pre_g, post_g, layer, wup, cw, cb, wdn):
    n, d = x.shape
    tf = FF_TILE
    hrows = CONV_W - 1
    state = lambda off: pl.BlockSpec((None, n, hrows, tf), lambda f: (layer, 0, 0, off + f))
    row3 = lambda off: pl.BlockSpec((CONV_W, tf), lambda f: (0, off + f))
    row1 = lambda off: pl.BlockSpec((1, tf), lambda f: (0, off + f))
    const = lambda shape: pl.BlockSpec(shape, lambda f: (0,) * len(shape))
    half = pl.BlockSpec((n, hrows, tf), lambda f: (0, 0, f))
    y, og, ou = pl.pallas_call(
        _ffn_sample_kernel,
        grid=(N_FF,),
        in_specs=[
            const((n, d)),
            state(0), state(N_FF),
            const((1, d)), const((1, d)),
            pl.BlockSpec((None, d, tf), lambda f: (layer, 0, f)),
            pl.BlockSpec((None, d, tf), lambda f: (layer, 0, N_FF + f)),
            row3(0), row3(N_FF), row1(0), row1(N_FF),
            pl.BlockSpec((None, tf, d), lambda f: (layer, f, 0)),
        ],
        out_specs=[const((n, d)), half, half],
        out_shape=[
            jax.ShapeDtypeStruct((n, d), F32),
            jax.ShapeDtypeStruct((n, hrows, D_FF), F32),
            jax.ShapeDtypeStruct((n, hrows, D_FF), F32),
        ],
        scratch_shapes=[pltpu.VMEM((n, d), BF16), pltpu.VMEM((n, d), F32)],
        compiler_params=_params(1),
        name="ffn_sample",
    )(x, state_all, state_all, pre_g, post_g, wup, wup, cw, cw, cb, cb, wdn)
    return y, jnp.concatenate([og, ou], axis=-1)


def _sgu_prompt_kernel(x_ref, pre_g_ref, post_g_ref, win_ref, sgug_ref, ws_ref, bs_ref, wout_ref,
                       y_ref, v_ref, gated_ref, h_ref):
    tm = x_ref.shape[1]
    gd = SGU_GROUP_DIM
    x = x_ref[0]

    @pl.when(pl.program_id(1) >= 0)
    def _():
        h_ref[...] = _rms(x, pre_g_ref[...]).astype(BF16)

    ss = jnp.zeros((tm, 1), F32)
    for j in range(SGU_GROUPS):
        zc = _gelu_erf(_dot(h_ref[...], win_ref[:, SGU_WIDTH + j * gd:SGU_WIDTH + (j + 1) * gd]))
        v_ref[:, j * gd:(j + 1) * gd] = zc
        ss = ss + jnp.sum(zc * zc, axis=-1, keepdims=True)
    inv = lax.rsqrt(ss * (1.0 / SGU_WIDTH) + EPS)
    causal = (lax.broadcasted_iota(jnp.int32, (CHUNK, CHUNK), 0)
              >= lax.broadcasted_iota(jnp.int32, (CHUNK, CHUNK), 1))
    for g in range(SGU_GROUPS):
        sl = slice(g * gd, (g + 1) * gd)
        vn = (v_ref[:, sl] * inv * sgug_ref[:, sl]).astype(BF16)
        wm = jnp.where(causal, ws_ref[g], 0.0).astype(BF16)
        bias = bs_ref[g]
        s = jnp.concatenate(
            [_dot(wm, vn[c * CHUNK:(c + 1) * CHUNK]) + bias for c in range(tm // CHUNK)], axis=0)
        u = _gelu_erf(_dot(h_ref[...], win_ref[:, sl]))
        gated_ref[:, sl] = (u * s).astype(BF16)
    y_ref[0] = x + _rms(_dot(gated_ref[...], wout_ref[...]), post_g_ref[...])


def _sgu_prompt(x, pre_g, post_g, layer, win, sgug, ws, bs, wout):
    b, t, d = x.shape
    tm = TM_SGU
    return pl.pallas_call(
        _sgu_prompt_kernel,
        grid=(b, t // tm),
        in_specs=[
            pl.BlockSpec((1, tm, d), lambda bi, i: (bi, i, 0)),
            _resident((1, d)), _resident((1, d)),
            _layer_resident(win, layer), _resident(sgug.shape), _resident(ws.shape), _resident(bs.shape),
            _layer_resident(wout, layer),
        ],
        out_specs=pl.BlockSpec((1, tm, d), lambda bi, i: (bi, i, 0)),
        out_shape=jax.ShapeDtypeStruct((b, t, d), F32),
        scratch_shapes=[pltpu.VMEM((tm, SGU_WIDTH), F32), pltpu.VMEM((tm, SGU_WIDTH), BF16),
                        pltpu.VMEM((tm, d), BF16)],
        compiler_params=_params(2),
        name="sgu_prompt",
    )(x, pre_g, post_g, win, sgug, ws, bs, wout)


def _sgu_sample_kernel(x_ref, pre_g_ref, post_g_ref, win_ref, sgug_ref, wdiag_ref, bdiag_ref, wout_ref,
                       y_ref, v_ref):
    x = x_ref[...]
    h = _rms(x, pre_g_ref[...]).astype(BF16)
    u = _gelu_erf(_dot(h, win_ref[:, :SGU_WIDTH]))
    v = _gelu_erf(_dot(h, win_ref[:, SGU_WIDTH:]))
    vn = _rms(v, sgug_ref[...])
    v_ref[...] = vn
    s = vn * wdiag_ref[...] + bdiag_ref[...]
    y = _dot((u * s).astype(BF16), wout_ref[...])
    y_ref[...] = x + _rms(y, post_g_ref[...])


def _sgu_sample(x, pre_g, post_g, layer, win, sgug, wdiag, bdiag, wout):
    n, d = x.shape
    whole = lambda shape: pl.BlockSpec(shape, lambda i: (0,) * len(shape))
    return pl.pallas_call(
        _sgu_sample_kernel,
        grid=(1,),
        in_specs=[whole((n, d)), whole((1, d)), whole((1, d)), _layer_resident(win, layer),
                  whole((1, SGU_WIDTH)), whole((1, SGU_WIDTH)), whole((1, SGU_WIDTH)),
                  _layer_resident(wout, layer)],
        out_specs=[whole((n, d)), whole((n, SGU_WIDTH))],
        out_shape=[jax.ShapeDtypeStruct((n, d), F32), jax.ShapeDtypeStruct((n, SGU_WIDTH), F32)],
        compiler_params=_params(1),
        name="sgu_sample",
    )(x, pre_g, post_g, win, sgug, wdiag, bdiag, wout)


def _kv_side_kernel(x_ref, cs_ref, g_in_ref, wc_ref, wpe_ref, kvg_ref, wuk_ref,
                    ckv_ref, kr_ref, kcat_ref, ckvt_ref, khead_ref):
    h = _rms(x_ref[0], g_in_ref[...]).astype(BF16)
    ckv = _rms(_dot(h, wc_ref[...]), kvg_ref[...])
    t = _dot(h, wpe_ref[...]) * cs_ref[...]
    t = t + pltpu.roll(t, QK_ROPE, 1)
    lane = lax.broadcasted_iota(jnp.int32, t.shape, 1)
    krp = jnp.where(lane < QK_ROPE, t, 0.0)
    ckv_ref[0] = ckv
    kr_ref[0] = t.T[:QK_ROPE]
    kcat_ref[0] = jnp.concatenate([ckv, krp], axis=1).astype(BF16)
    kb = ckvt_ref.shape[-1]
    ones_row = (lax.broadcasted_iota(jnp.int32, (VROWS - KV_LORA, kb), 0) == 0).astype(F32)
    for j in range(ckvt_ref.shape[1]):
        ckvt_ref[0, j] = jnp.concatenate([ckv[j * kb:(j + 1) * kb].T, ones_row], axis=0).astype(BF16)
    kn = _dot(ckv.astype(BF16), wuk_ref[...])
    for hd in range(N_HEADS):
        khead_ref[0, hd] = jnp.concatenate(
            [kn[:, hd * QK_NOPE:(hd + 1) * QK_NOPE], krp], axis=1).astype(BF16)


def _kv_side(x, cs_tab, g_in, wc, wpe, kvg, wuk, tm):
    b, t, d = x.shape
    kb = min(tm, TQ)
    return pl.pallas_call(
        _kv_side_kernel,
        grid=(b, t // tm),
        in_specs=[
            pl.BlockSpec((1, tm, d), lambda bi, i: (bi, i, 0)),
            pl.BlockSpec((tm, LANES), lambda bi, i: (i, 0)),
            _resident((1, d)), _resident(wc.shape), _resident(wpe.shape), _resident((1, KV_LORA)),
            _resident(wuk.shape),
        ],
        out_specs=[
            pl.BlockSpec((1, tm, KV_LORA), lambda bi, i: (bi, i, 0)),
            pl.BlockSpec((1, QK_ROPE, tm), lambda bi, i: (bi, 0, i)),
            pl.BlockSpec((1, tm, KCAT), lambda bi, i: (bi, i, 0)),
            pl.BlockSpec((1, tm // kb, VROWS, kb), lambda bi, i: (bi, i, 0, 0)),
            pl.BlockSpec((1, N_HEADS, tm, KHEAD), lambda bi, i: (bi, 0, i, 0)),
        ],
        out_shape=[
            jax.ShapeDtypeStruct((b, t, KV_LORA), F32),
            jax.ShapeDtypeStruct((b, QK_ROPE, t), F32),
            jax.ShapeDtypeStruct((b, t, KCAT), BF16),
            jax.ShapeDtypeStruct((b, t // kb, VROWS, kb), BF16),
            jax.ShapeDtypeStruct((b, N_HEADS, t, KHEAD), BF16),
        ],
        compiler_params=_params(2),
        name="kv_side",
    )(x, cs_tab, g_in, wc, wpe, kvg, wuk)


def _q_rows(x, cos, sin, pre_g, wdq, qg, wuq_n, wuq_pa, wuq_pb, wuk_t, store):
    h = _rms(x, pre_g).astype(BF16)
    cq = _rms(_dot(h, wdq[...]), qg).astype(BF16)
    qn = _dot(cq, wuq_n[...])
    qa = _dot(cq, wuq_pa[...])
    qb = _dot(cq, wuq_pb[...])
    for hd in range(N_HEADS):
        sl = slice(hd * LANES, (hd + 1) * LANES)
        qpe = qa[:, sl] * cos + qb[:, sl] * sin
        ql = _dot(qn[:, sl].astype(BF16), wuk_t[hd])
        store(hd, (ql * ATTN_SCALE).astype(BF16), (qpe * ATTN_SCALE).astype(BF16))


def _mla_out(o_heads, x, post_g, wuv, wo, o_ref):
    for hd in range(N_HEADS):
        o_ref[:, hd * V_HEAD:(hd + 1) * V_HEAD] = _dot(o_heads(hd).astype(BF16), wuv[hd]).astype(BF16)
    m = _dot(o_ref[...], wo[...])
    return x + _rms(m, post_g)


def _mla_prompt_kernel(x_ref, k_ref, vt_ref, cos_ref, sin_ref, pre_g_ref, post_g_ref, wdq_ref, qg_ref,
                       wuqn_ref, wuqa_ref, wuqb_ref, wuvt_ref, wot_ref,
                       y_ref, qt_ref, m_ref, acc_ref, o_ref):
    tq = x_ref.shape[1]
    qi = pl.program_id(1)
    x = x_ref[0]

    h = _rms(x, pre_g_ref[...]).astype(BF16)
    cq = _rms(_dot(h, wdq_ref[...]), qg_ref[...])
    cq_t = cq.T.astype(BF16)
    qn_t = _dot(wuqn_ref[...], cq_t)
    qa_t = _dot(wuqa_ref[...], cq_t)
    qb_t = _dot(wuqb_ref[...], cq_t)
    cos_t, sin_t = cos_ref[...], sin_ref[...]
    for hd in range(N_HEADS):
        cols = slice(hd * tq, (hd + 1) * tq)
        rows = slice(hd * QK_ROPE, (hd + 1) * QK_ROPE)
        qpe_t = qa_t[rows] * cos_t + qb_t[rows] * sin_t
        qt_ref[:QK_NOPE, cols] = (qn_t[hd * QK_NOPE:(hd + 1) * QK_NOPE] * ATTN_SCALE).astype(BF16)
        qt_ref[QK_NOPE:QK_NOPE + QK_ROPE, cols] = (qpe_t * ATTN_SCALE).astype(BF16)
    qt_ref[QK_NOPE + QK_ROPE:, :] = jnp.zeros((KHEAD - QK_NOPE - QK_ROPE, N_HEADS * tq), BF16)

    def step(kb, diagonal):
        keys = pl.ds(pl.multiple_of(kb * tq, tq), tq)
        vt = vt_ref[0, kb]
        scores = [_dot(k_ref[0, hd, keys, :], qt_ref[:, hd * tq:(hd + 1) * tq])
                  for hd in range(N_HEADS)]
        for hd in range(N_HEADS):
            cols = slice(hd * tq, (hd + 1) * tq)
            st = scores[hd]
            if diagonal:
                key = lax.broadcasted_iota(jnp.int32, st.shape, 0)
                qry = lax.broadcasted_iota(jnp.int32, st.shape, 1)
                st = jnp.where(key <= qry, st, -jnp.inf)
                m_new = jnp.max(st, axis=0, keepdims=True)
                p = jnp.exp(st - m_new)
                acc_ref[:, cols] = _dot(vt, p.astype(BF16))
            else:
                m_old = m_ref[hd:hd + 1, :]
                m_new = jnp.maximum(m_old, jnp.max(st, axis=0, keepdims=True))
                alpha = jnp.exp(m_old - m_new)
                p = jnp.exp(st - m_new)
                acc_ref[:, cols] = acc_ref[:, cols] * alpha + _dot(vt, p.astype(BF16))
            m_ref[hd:hd + 1, :] = m_new

    def body(kb, carry):
        step(kb, False)
        return carry

    step(qi, True)
    lax.fori_loop(0, qi, body, 0)

    for hd in range(N_HEADS):
        cols = slice(hd * tq, (hd + 1) * tq)
        o_t = acc_ref[:KV_LORA, cols] * (1.0 / acc_ref[KV_LORA:KV_LORA + 1, cols])
        o_ref[hd * V_HEAD:(hd + 1) * V_HEAD, :] = _dot(wuvt_ref[hd], o_t.astype(BF16)).astype(BF16)
    m_t = _dot(wot_ref[...], o_ref[...])
    y_ref[0] = x + _rms(m_t.T, post_g_ref[...])


def _mla_prompt(x, khead, ckv_t, cos_t, sin_t, pre_g, post_g, wdq, qg, wuqn_t, wuqa_t, wuqb_t,
                wuvt_h, wo_t):
    b, t, d = x.shape
    tq = TQ
    lanes = N_HEADS * tq
    weights = (wdq, qg, wuqn_t, wuqa_t, wuqb_t, wuvt_h, wo_t)
    return pl.pallas_call(
        _mla_prompt_kernel,
        grid=(b, t // tq),
        in_specs=[
            pl.BlockSpec((1, tq, d), lambda bi, i: (bi, i, 0)),
            pl.BlockSpec((1, N_HEADS, t, KHEAD), lambda bi, i: (bi, 0, 0, 0)),
            pl.BlockSpec((1, t // tq, VROWS, tq), lambda bi, i: (bi, 0, 0, 0)),
            pl.BlockSpec((QK_ROPE, tq), lambda bi, i: (0, i)),
            pl.BlockSpec((QK_ROPE, tq), lambda bi, i: (0, i)),
            _resident((1, d)), _resident((1, d)),
        ] + [_resident(w.shape) for w in weights],
        out_specs=pl.BlockSpec((1, tq, d), lambda bi, i: (bi, i, 0)),
        out_shape=jax.ShapeDtypeStruct((b, t, d), F32),
        scratch_shapes=[
            pltpu.VMEM((KHEAD, lanes), BF16),
            pltpu.VMEM((N_HEADS, tq), F32),
            pltpu.VMEM((VROWS, lanes), F32),
            pltpu.VMEM((N_HEADS * V_HEAD, tq), BF16),
        ],
        compiler_params=_params(2),
        name="mla_prompt",
    )(x, khead, ckv_t, cos_t, sin_t, pre_g, post_g, *weights)


def _q_sample_kernel(x_ref, cos_ref, sin_ref, pre_g_ref, wdq_ref, qg_ref, wuqn_ref, wuqa_ref, wuqb_ref,
                     wukt_ref, q_ref):
    def store(hd, ql, qpe):
        q_ref[hd, :, :KV_LORA] = ql
        q_ref[hd, :, KV_LORA:] = qpe

    _q_rows(x_ref[...], cos_ref[...], sin_ref[...], pre_g_ref[...], wdq_ref, qg_ref[...],
            wuqn_ref, wuqa_ref, wuqb_ref, wukt_ref, store)


def _q_sample(x, cos, sin, pre_g, wdq, qg, wuq_n, wuq_pa, wuq_pb, wuk_t):
    n = x.shape[0]
    return pl.pallas_call(
        _q_sample_kernel,
        out_shape=jax.ShapeDtypeStruct((N_HEADS, n, KCAT), BF16),
        compiler_params=pltpu.CompilerParams(vmem_limit_bytes=VMEM_LIMIT),
        name="q_sample",
    )(x, cos, sin, pre_g, wdq, qg, wuq_n, wuq_pa, wuq_pb, wuk_t)


def _decode_kernel(pt_ref, q_ref, knew_ref, ckv_hbm, krt_hbm, o_ref, cbuf, rbuf, sem):
    b = pl.program_id(0)
    nb = pl.num_programs(0)
    n_pages = cbuf.shape[1]
    n_chunks = n_pages // DEC_CHUNK_PAGES
    cp_rows = DEC_CHUNK_PAGES * PAGE_SIZE
    ahead = DEC_SLOTS - 1

    def page_copies(page, slot, p):
        return (pltpu.make_async_copy(ckv_hbm.at[page], cbuf.at[slot, p], sem.at[0, slot]),
                pltpu.make_async_copy(krt_hbm.at[page], rbuf.at[slot, p], sem.at[1, slot]))

    def start_pages(seq, slot, p_lo, p_hi):
        for p in range(p_lo, p_hi):
            for cp in page_copies(pt_ref[seq, p], slot, p):
                cp.start(priority=p % 2)

    def wait_slot(slot):
        for p in range(n_pages):
            for cp in page_copies(0, slot, p):
                cp.wait()

    slot = b % DEC_SLOTS
    nxt = jnp.minimum(b + ahead, nb - 1)
    nxt_slot = (b + ahead) % DEC_SLOTS

    @pl.when(b == 0)
    def _():
        for s in range(ahead):
            start_pages(s, s, 0, n_pages)

    wait_slot(slot)

    q = q_ref[0]
    ql = q[:, :KV_LORA]
    qp = q[:, KV_LORA:KV_LORA + QK_ROPE]

    def chunk_scores(c):
        start_pages(nxt, nxt_slot, c * DEC_CHUNK_PAGES, (c + 1) * DEC_CHUNK_PAGES)
        s_pe = jnp.concatenate(
            [_dot(qp, rbuf[slot, c * DEC_CHUNK_PAGES + j].astype(BF16)) for j in range(DEC_CHUNK_PAGES)],
            axis=1)
        rows = cbuf[slot, c * DEC_CHUNK_PAGES:(c + 1) * DEC_CHUNK_PAGES].reshape(cp_rows, KV_LORA)
        return _dot(ql, rows.T.astype(BF16)) + s_pe, rows.astype(BF16)

    kn = knew_ref[0].astype(F32)
    m = jnp.sum(q.astype(F32) * kn, axis=-1, keepdims=True)
    l = jnp.ones_like(m)
    acc = jnp.broadcast_to(kn[:, :KV_LORA], (N_HEADS, KV_LORA))
    nxt_chunk = chunk_scores(0)
    for c in range(n_chunks):
        s, kc = nxt_chunk
        if c + 1 < n_chunks:
            nxt_chunk = chunk_scores(c + 1)
        m_new = jnp.maximum(m, jnp.max(s, axis=-1, keepdims=True))
        alpha = jnp.exp(m - m_new)
        p = jnp.exp(s - m_new)
        l = l * alpha + jnp.sum(p, axis=-1, keepdims=True)
        acc = acc * alpha + _dot(p.astype(BF16), kc)
        m = m_new
    o_ref[0] = acc / l

    @pl.when(b == nb - 1)
    def _():
        for s in range(1, DEC_SLOTS):
            wait_slot((slot + s) % DEC_SLOTS)


def _decode(page_table, q, knew, cache_ckv, cache_krt):
    n, n_pages = page_table.shape
    grid_spec = pltpu.PrefetchScalarGridSpec(
        num_scalar_prefetch=1,
        grid=(n,),
        in_specs=[
            pl.BlockSpec((1, N_HEADS, KCAT), lambda bi, pt: (bi, 0, 0)),
            pl.BlockSpec((1, 1, KCAT), lambda bi, pt: (bi, 0, 0)),
            pl.BlockSpec(memory_space=pl.ANY),
            pl.BlockSpec(memory_space=pl.ANY),
        ],
        out_specs=pl.BlockSpec((1, N_HEADS, KV_LORA), lambda bi, pt: (bi, 0, 0)),
        scratch_shapes=[
            pltpu.VMEM((DEC_SLOTS, n_pages, PAGE_SIZE, KV_LORA), F32),
            pltpu.VMEM((DEC_SLOTS, n_pages, QK_ROPE, PAGE_SIZE), F32),
            pltpu.SemaphoreType.DMA((2, DEC_SLOTS)),
        ],
    )
    return pl.pallas_call(
        _decode_kernel,
        grid_spec=grid_spec,
        out_shape=jax.ShapeDtypeStruct((n, N_HEADS, KV_LORA), F32),
        compiler_params=_params(1),
        name="mla_decode",
    )(page_table, q, knew, cache_ckv, cache_krt)


def _out_sample_kernel(o_ref, x_ref, post_g_ref, wuv_ref, wo_ref, y_ref, os_ref):
    y_ref[...] = _mla_out(lambda hd: o_ref[hd], x_ref[...], post_g_ref[...], wuv_ref, wo_ref, os_ref)


def _out_sample(o_heads, x, post_g, wuv, wo):
    n, d = x.shape
    return pl.pallas_call(
        _out_sample_kernel,
        out_shape=jax.ShapeDtypeStruct((n, d), F32),
        scratch_shapes=[pltpu.VMEM((n, N_HEADS * V_HEAD), BF16)],
        compiler_params=pltpu.CompilerParams(vmem_limit_bytes=VMEM_LIMIT),
        name="out_sample",
    )(o_heads, x, post_g, wuv, wo)


def _rot_half_cols(w):
    half = QK_ROPE // 2
    return jnp.concatenate([-w[..., half:], w[..., :half]], axis=-1)


def _rope_tables(pos):
    half = QK_ROPE // 2
    inv = 1.0 / (ROPE_THETA ** (jnp.arange(half, dtype=F32) / half))
    ang = pos.astype(F32)[:, None] * inv[None, :]
    cos, sin = jnp.cos(ang), jnp.sin(ang)
    cos64 = jnp.concatenate([cos, cos], axis=-1)
    sin64 = jnp.concatenate([sin, sin], axis=-1)
    return cos64, sin64


def kernel(x_prompt, x_sample, cache_ckv, cache_kr, state_conv, page_table, pre_mix_g, post_mix_g,
           pre_ffn_g, post_ffn_g, w_in_a, sgu_g, w_s, b_s, w_out_a, kv_in_g, w_dkv, kv_g, w_uk, w_uv,
           w_dq, q_g, w_uq, w_o, w_up, conv_w, conv_b, w_down):
    depth = w_up.shape[0]
    n_a = w_in_a.shape[0]
    bp, tp, _ = x_prompt.shape
    ns = x_sample.shape[0]
    past_len = page_table.shape[1] * PAGE_SIZE
    row = lambda v: v.reshape(1, -1)

    w_in_b, w_out_b = w_in_a.astype(BF16), w_out_a.astype(BF16)
    w_up_b, w_down_b = w_up.astype(BF16), w_down.astype(BF16)
    w_dq_b, w_o_b = w_dq.astype(BF16), w_o.astype(BF16)
    wuq = w_uq.reshape(-1, Q_LORA, N_HEADS, QK_NOPE + QK_ROPE)
    wuq_n = wuq[..., :QK_NOPE].reshape(-1, Q_LORA, N_HEADS * QK_NOPE).astype(BF16)
    pad = lambda w: jnp.pad(w, ((0, 0),) * 3 + ((0, LANES - QK_ROPE),)).reshape(
        -1, Q_LORA, N_HEADS * LANES).astype(BF16)
    wuq_pa = pad(wuq[..., QK_NOPE:])
    wuq_pb = pad(_rot_half_cols(wuq[..., QK_NOPE:]))
    wuk_t = jnp.transpose(w_uk, (1, 2, 0)).astype(BF16)
    wuv_h = jnp.transpose(w_uv, (1, 0, 2)).astype(BF16)
    wuqn_t = jnp.swapaxes(wuq_n, 1, 2)
    flat_t = lambda w: jnp.swapaxes(w.reshape(-1, Q_LORA, N_HEADS * QK_ROPE), 1, 2).astype(BF16)
    wuqa_t = flat_t(wuq[..., QK_NOPE:])
    wuqb_t = flat_t(_rot_half_cols(wuq[..., QK_NOPE:]))
    wuk_all = w_uk.reshape(KV_LORA, N_HEADS * QK_NOPE).astype(BF16)
    wuvt_h = jnp.transpose(w_uv, (1, 2, 0)).astype(BF16)
    wo_t = jnp.swapaxes(w_o_b, 1, 2)
    cache_krt = jnp.swapaxes(cache_kr, 1, 2)
    w_c = w_dkv[:, :KV_LORA].astype(BF16)
    w_pe = w_dkv[:, KV_LORA:]
    w_pe2 = jnp.concatenate([w_pe, _rot_half_cols(w_pe)], axis=-1).astype(BF16)
    bs_col = b_s[..., None]
    wdiag = jnp.repeat(w_s[:, :, 0, 0], SGU_GROUP_DIM, axis=-1)
    bdiag = jnp.repeat(b_s[:, :, 0], SGU_GROUP_DIM, axis=-1)

    cos_p, sin_p = _rope_tables(jnp.arange(tp, dtype=jnp.int32))
    cos_s, sin_s = _rope_tables(jnp.full((ns,), past_len, dtype=jnp.int32))
    tile2 = lambda a: jnp.concatenate([a, a], axis=-1)
    cs_p = jnp.concatenate([cos_p, sin_p], axis=-1)
    cs_s = jnp.concatenate([cos_s, sin_s], axis=-1)

    xp = x_prompt
    xs = x_sample.reshape(ns, D_MODEL)
    conv_p, conv_s, v_rows = [], [], []
    khead_p = kcat_s = ckvt_p = ckv_p = kr_p = ckv_s = kr_s = None
    for layer in range(depth):
        if layer == n_a:
            kv_w = (row(kv_in_g), w_c, w_pe2, row(kv_g), wuk_all)
            ckv_p, kr_p, _, ckvt_p, khead_p = _kv_side(xp, cs_p, *kv_w, TM_KV)
            ckv_s, kr_s, kcat_s, _, _ = _kv_side(xs[None], cs_s, *kv_w, ns)
        pre_g, post_g = row(pre_mix_g[layer]), row(post_mix_g[layer])
        if layer < n_a:
            xp = _sgu_prompt(xp, pre_g, post_g, layer, w_in_b, row(sgu_g[layer]), w_s[layer],
                             bs_col[layer], w_out_b)
            xs, v = _sgu_sample(xs, pre_g, post_g, layer, w_in_b, row(sgu_g[layer]), row(wdiag[layer]),
                                row(bdiag[layer]), w_out_b)
            v_rows.append(v)
        else:
            j = layer - n_a
            qw = (w_dq_b[j], row(q_g[j]), wuq_n[j], wuq_pa[j], wuq_pb[j], wuk_t)
            xp = _mla_prompt(xp, khead_p, ckvt_p, cos_p.T, sin_p.T, pre_g, post_g, w_dq_b[j], row(q_g[j]),
                             wuqn_t[j], wuqa_t[j], wuqb_t[j], wuvt_h, wo_t[j])
            q = _q_sample(xs, tile2(cos_s), tile2(sin_s), pre_g, *qw)
            o = _decode(page_table, jnp.transpose(q, (1, 0, 2)), kcat_s.reshape(ns, 1, KCAT),
                        cache_ckv, cache_krt)
            xs = _out_sample(jnp.transpose(o, (1, 0, 2)), xs, post_g, wuv_h, w_o_b[j])
        pre_g, post_g = row(pre_ffn_g[layer]), row(post_ffn_g[layer])
        xp, tail = _ffn_prompt(xp, pre_g, post_g, layer, w_up_b, conv_w[layer], row(conv_b[layer]),
                               w_down_b)
        conv_p.append(tail)
        xs, st = _ffn_sample(xs, state_conv, pre_g, post_g, layer, w_up_b, conv_w[layer],
                             row(conv_b[layer]), w_down_b)
        conv_s.append(st)

    return (xp, xs.reshape(ns, 1, D_MODEL), ckv_p, jnp.swapaxes(kr_p, 1, 2), jnp.stack(conv_p),
            ckv_s.reshape(ns, 1, KV_LORA), jnp.swapaxes(kr_s, 1, 2).reshape(ns, 1, QK_ROPE), jnp.stack(conv_s),
            jnp.stack(v_rows).reshape(n_a, ns, 1, SGU_WIDTH))
```

```python
import jax
import jax.numpy as jnp
from jax import lax
from jax.experimental import pallas as pl
from jax.experimental.pallas import tpu as pltpu

F32 = jnp.float32
BF16 = jnp.bfloat16

D_MODEL = 1024
N_A_LAYERS = 2
CHUNK = 128
SGU_WIDTH = 2 * D_MODEL
SGU_GROUPS = 8
SGU_GROUP_DIM = SGU_WIDTH // SGU_GROUPS
N_HEADS = 8
QK_NOPE = 128
QK_ROPE = 64
V_HEAD = 128
Q_LORA = D_MODEL // 2
KV_LORA = D_MODEL // 4
ROPE_THETA = 10000.0
ATTN_SCALE = (QK_NOPE + QK_ROPE) ** -0.5
D_FF = 11 * D_MODEL // 4
CONV_W = 3
EPS = 1e-6
PAGE_SIZE = 128

LANES = 128
BF16_ROWS = 16
VMEM_LIMIT = 56 * 1024 * 1024

KCAT = KV_LORA + LANES
KHEAD = QK_NOPE + LANES
VROWS = KV_LORA + BF16_ROWS
FF_TILE = 256
N_FF = D_FF // FF_TILE
TM_FFN = 512
TM_SGU = 512
TQ = 256
TM_KV = 1024
DEC_CHUNK_PAGES = 8
DEC_SLOTS = 3


def _rms(x, g):
    return x * lax.rsqrt(jnp.mean(x * x, axis=-1, keepdims=True) + EPS) * g


def _gelu_erf(x):
    return 0.5 * x * (1.0 + lax.erf(x * (0.5 ** 0.5)))


def _dot(a, b):
    return jnp.dot(a, b, preferred_element_type=F32)


def _dot_nt(a, b):
    return lax.dot_general(a, b, (((1,), (1,)), ((), ())), preferred_element_type=F32)


def _params(n_grid_axes):
    return pltpu.CompilerParams(
        dimension_semantics=("arbitrary",) * n_grid_axes, vmem_limit_bytes=VMEM_LIMIT)


def _resident(shape):
    nd = len(shape)
    return pl.BlockSpec(shape, lambda *_: (0,) * nd, pipeline_mode=pl.Buffered(1))


def _layer_resident(stacked, layer):
    nd = stacked.ndim - 1
    return pl.BlockSpec((None,) + stacked.shape[1:], lambda *_: (layer,) + (0,) * nd,
                        pipeline_mode=pl.Buffered(1))


def _ffn_prompt_kernel(x_ref, pre_g_ref, post_g_ref, wup_ref, cw_ref, cb_ref, wdn_ref,
                       y_ref, tail_ref, act_ref, h_ref):
    tm = x_ref.shape[1]
    ng = tm // 8
    i = pl.program_id(1)

    @pl.when(i == 0)
    def _():
        tail_ref[...] = jnp.zeros(tail_ref.shape, F32)

    @pl.when(i >= 0)
    def _():
        hn = _rms(x_ref[0], pre_g_ref[...])
        h_ref[...] = jnp.swapaxes(hn.reshape(8, ng, hn.shape[-1]), 0, 1).reshape(tm, -1).astype(BF16)

    first = lax.broadcasted_iota(jnp.int32, (8, FF_TILE), 0) == 0

    def conv(col0):
        sl = slice(col0, col0 + FF_TILE)
        ap = _dot(h_ref[...], wup_ref[:, sl])
        prev = tail_ref[0, :, sl]
        sh1 = jnp.where(first, pltpu.roll(prev[8:], 1, 0), pltpu.roll(ap[tm - 8:], 1, 0))
        sh2 = jnp.where(first, pltpu.roll(prev[:8], 1, 0), pltpu.roll(ap[tm - 16:tm - 8], 1, 0))
        w = cw_ref[:, sl]
        c = cb_ref[:, sl] + jnp.concatenate([sh2, sh1, ap[:tm - 16]], axis=0) * w[0:1]
        c = c + jnp.concatenate([sh1, ap[:tm - 8]], axis=0) * w[1:2]
        c = c + ap * w[2:3]
        tail_ref[0, :, sl] = ap[tm - 16:]
        return c

    for f in range(N_FF):
        cg = conv(f * FF_TILE)
        cu = conv(D_FF + f * FF_TILE)
        act_ref[:, f * FF_TILE:(f + 1) * FF_TILE] = (jax.nn.gelu(cg, approximate=True) * cu).astype(BF16)
    fo = _rms(_dot(act_ref[...], wdn_ref[...]), post_g_ref[...])
    y_ref[0] = x_ref[0] + jnp.swapaxes(fo.reshape(ng, 8, fo.shape[-1]), 0, 1).reshape(tm, -1)


def _ffn_prompt(x, pre_g, post_g, layer, wup, cw, cb, wdn):
    b, t, d = x.shape
    tm = TM_FFN
    y, tail = pl.pallas_call(
        _ffn_prompt_kernel,
        grid=(b, t // tm),
        in_specs=[
            pl.BlockSpec((1, tm, d), lambda bi, i: (bi, i, 0)),
            _resident((1, d)), _resident((1, d)),
            _layer_resident(wup, layer), _resident(cw.shape), _resident(cb.shape),
            _layer_resident(wdn, layer),
        ],
        out_specs=[
            pl.BlockSpec((1, tm, d), lambda bi, i: (bi, i, 0)),
            pl.BlockSpec((1, 16, 2 * D_FF), lambda bi, i: (bi, 0, 0)),
        ],
        out_shape=[
            jax.ShapeDtypeStruct((b, t, d), F32),
            jax.ShapeDtypeStruct((b, 16, 2 * D_FF), F32),
        ],
        scratch_shapes=[pltpu.VMEM((tm, D_FF), BF16), pltpu.VMEM((tm, d), BF16)],
        compiler_params=_params(2),
        name="ffn_prompt",
    )(x, pre_g, post_g, wup, cw, cb, wdn)
    return y, tail[:, 7::8, :]


def _ffn_sample_kernel(x_ref, sg_ref, su_ref, pre_g_ref, post_g_ref,
                       wg_ref, wu_ref, cwg_ref, cwu_ref, cbg_ref, cbu_ref, wdn_ref,
                       y_ref, og_ref, ou_ref, h_ref, acc_ref):
    f = pl.program_id(0)

    @pl.when(f == 0)
    def _():
        h_ref[...] = _rms(x_ref[...], pre_g_ref[...]).astype(BF16)

    h = h_ref[...]

    def conv(w_ref, s_ref, cw_ref, cb_ref, o_ref):
        a = _dot(h, w_ref[...])
        p0, p1 = s_ref[:, 0, :], s_ref[:, 1, :]
        o_ref[:, 0, :] = p1
        o_ref[:, 1, :] = a
        w = cw_ref[...]
        c = cb_ref[...] + p0 * w[0:1]
        c = c + p1 * w[1:2]
        return c + a * w[2:3]

    cg = conv(wg_ref, sg_ref, cwg_ref, cbg_ref, og_ref)
    cu = conv(wu_ref, su_ref, cwu_ref, cbu_ref, ou_ref)
    act = (jax.nn.gelu(cg, approximate=True) * cu).astype(BF16)
    part = _dot(act, wdn_ref[...])

    @pl.when(f == 0)
    def _():
        acc_ref[...] = part

    @pl.when(f > 0)
    def _():
        acc_ref[...] += part

    @pl.when(f == pl.num_programs(0) - 1)
    def _():
        y_ref[...] = x_ref[...] + _rms(acc_ref[...], post_g_ref[...])


def _ffn_sample(x, state_all, pre_g, post_g, layer, wup, cw, cb, wdn):
    n, d = x.shape
    tf = FF_TILE
    hrows = CONV_W - 1
    state = lambda off: pl.BlockSpec((None, n, hrows, tf), lambda f: (layer, 0, 0, off + f))
    row3 = lambda off: pl.BlockSpec((CONV_W, tf), lambda f: (0, off + f))
    row1 = lambda off: pl.BlockSpec((1, tf), lambda f: (0, off + f))
    const = lambda shape: pl.BlockSpec(shape, lambda f: (0,) * len(shape))
    half = pl.BlockSpec((n, hrows, tf), lambda f: (0, 0, f))
    y, og, ou = pl.pallas_call(
        _ffn_sample_kernel,
        grid=(N_FF,),
        in_specs=[
            const((n, d)),
            state(0), state(N_FF),
            const((1, d)), const((1, d)),
            pl.BlockSpec((None, d, tf), lambda f: (layer, 0, f)),
            pl.BlockSpec((None, d, tf), lambda f: (layer, 0, N_FF + f)),
            row3(0), row3(N_FF), row1(0), row1(N_FF),
            pl.BlockSpec((None, tf, d), lambda f: (layer, f, 0)),
        ],
        out_specs=[const((n, d)), half, half],
        out_shape=[
            jax.ShapeDtypeStruct((n, d), F32),
            jax.ShapeDtypeStruct((n, hrows, D_FF), F32),
            jax.ShapeDtypeStruct((n, hrows, D_FF), F32),
        ],
        scratch_shapes=[pltpu.VMEM((n, d), BF16), pltpu.VMEM((n, d), F32)],
        compiler_params=_params(1),
        name="ffn_sample",
    )(x, state_all, state_all, pre_g, post_g, wup, wup, cw, cw, cb, cb, wdn)
    return y, jnp.concatenate([og, ou], axis=-1)


def _sgu_prompt_kernel(x_ref, pre_g_ref, post_g_ref, win_ref, sgug_ref, ws_ref, bs_ref, wout_ref,
                       y_ref, v_ref, gated_ref, h_ref):
    tm = x_ref.shape[1]
    gd = SGU_GROUP_DIM
    x = x_ref[0]

    @pl.when(pl.program_id(1) >= 0)
    def _():
        h_ref[...] = _rms(x, pre_g_ref[...]).astype(BF16)

    ss = jnp.zeros((tm, 1), F32)
    for j in range(SGU_GROUPS):
        zc = _gelu_erf(_dot(h_ref[...], win_ref[:, SGU_WIDTH + j * gd:SGU_WIDTH + (j + 1) * gd]))
        v_ref[:, j * gd:(j + 1) * gd] = zc
        ss = ss + jnp.sum(zc * zc, axis=-1, keepdims=True)
    inv = lax.rsqrt(ss * (1.0 / SGU_WIDTH) + EPS)
    causal = (lax.broadcasted_iota(jnp.int32, (CHUNK, CHUNK), 0)
              >= lax.broadcasted_iota(jnp.int32, (CHUNK, CHUNK), 1))
    for g in range(SGU_GROUPS):
        sl = slice(g * gd, (g + 1) * gd)
        vn = (v_ref[:, sl] * inv * sgug_ref[:, sl]).astype(BF16)
        wm = jnp.where(causal, ws_ref[g], 0.0).astype(BF16)
        bias = bs_ref[g]
        s = jnp.concatenate(
            [_dot(wm, vn[c * CHUNK:(c + 1) * CHUNK]) + bias for c in range(tm // CHUNK)], axis=0)
        u = _gelu_erf(_dot(h_ref[...], win_ref[:, sl]))
        gated_ref[:, sl] = (u * s).astype(BF16)
    y_ref[0] = x + _rms(_dot(gated_ref[...], wout_ref[...]), post_g_ref[...])


def _sgu_prompt(x, pre_g, post_g, layer, win, sgug, ws, bs, wout):
    b, t, d = x.shape
    tm = TM_SGU
    return pl.pallas_call(
        _sgu_prompt_kernel,
        grid=(b, t // tm),
        in_specs=[
            pl.BlockSpec((1, tm, d), lambda bi, i: (bi, i, 0)),
            _resident((1, d)), _resident((1, d)),
            _layer_resident(win, layer), _resident(sgug.shape), _resident(ws.shape), _resident(bs.shape),
            _layer_resident(wout, layer),
        ],
        out_specs=pl.BlockSpec((1, tm, d), lambda bi, i: (bi, i, 0)),
        out_shape=jax.ShapeDtypeStruct((b, t, d), F32),
        scratch_shapes=[pltpu.VMEM((tm, SGU_WIDTH), F32), pltpu.VMEM((tm, SGU_WIDTH), BF16),
                        pltpu.VMEM((tm, d), BF16)],
        compiler_params=_params(2),
        name="sgu_prompt",
    )(x, pre_g, post_g, win, sgug, ws, bs, wout)


def _sgu_sample_kernel(x_ref, pre_g_ref, post_g_ref, win_ref, sgug_ref, wdiag_ref, bdiag_ref, wout_ref,
                       y_ref, v_ref):
    x = x_ref[...]
    h = _rms(x, pre_g_ref[...]).astype(BF16)
    u = _gelu_erf(_dot(h, win_ref[:, :SGU_WIDTH]))
    v = _gelu_erf(_dot(h, win_ref[:, SGU_WIDTH:]))
    vn = _rms(v, sgug_ref[...])
    v_ref[...] = vn
    s = vn * wdiag_ref[...] + bdiag_ref[...]
    y = _dot((u * s).astype(BF16), wout_ref[...])
    y_ref[...] = x + _rms(y, post_g_ref[...])


def _sgu_sample(x, pre_g, post_g, layer, win, sgug, wdiag, bdiag, wout):
    n, d = x.shape
    whole = lambda shape: pl.BlockSpec(shape, lambda i: (0,) * len(shape))
    return pl.pallas_call(
        _sgu_sample_kernel,
        grid=(1,),
        in_specs=[whole((n, d)), whole((1, d)), whole((1, d)), _layer_resident(win, layer),
                  whole((1, SGU_WIDTH)), whole((1, SGU_WIDTH)), whole((1, SGU_WIDTH)),
                  _layer_resident(wout, layer)],
        out_specs=[whole((n, d)), whole((n, SGU_WIDTH))],
        out_shape=[jax.ShapeDtypeStruct((n, d), F32), jax.ShapeDtypeStruct((n, SGU_WIDTH), F32)],
        compiler_params=_params(1),
        name="sgu_sample",
    )(x, pre_g, post_g, win, sgug, wdiag, bdiag, wout)


def _kv_side_kernel(x_ref, cs_ref, g_in_ref, wc_ref, wpe_ref, kvg_ref, wuk_ref,
                    ckv_ref, kr_ref, kcat_ref, ckvt_ref, khead_ref):
    h = _rms(x_ref[0], g_in_ref[...]).astype(BF16)
    ckv = _rms(_dot(h, wc_ref[...]), kvg_ref[...])
    t = _dot(h, wpe_ref[...]) * cs_ref[...]
    t = t + pltpu.roll(t, QK_ROPE, 1)
    lane = lax.broadcasted_iota(jnp.int32, t.shape, 1)
    krp = jnp.where(lane < QK_ROPE, t, 0.0)
    ckv_ref[0] = ckv
    kr_ref[0] = t.T[:QK_ROPE]
    kcat_ref[0] = jnp.concatenate([ckv, krp], axis=1).astype(BF16)
    kb = ckvt_ref.shape[-1]
    ones_row = (lax.broadcasted_iota(jnp.int32, (VROWS - KV_LORA, kb), 0) == 0).astype(F32)
    for j in range(ckvt_ref.shape[1]):
        ckvt_ref[0, j] = jnp.concatenate([ckv[j * kb:(j + 1) * kb].T, ones_row], axis=0).astype(BF16)
    kn = _dot(ckv.astype(BF16), wuk_ref[...])
    for hd in range(N_HEADS):
        khead_ref[0, hd] = jnp.concatenate(
            [kn[:, hd * QK_NOPE:(hd + 1) * QK_NOPE], krp], axis=1).astype(BF16)


def _kv_side(x, cs_tab, g_in, wc, wpe, kvg, wuk, tm):
    b, t, d = x.shape
    kb = min(tm, TQ)
    return pl.pallas_call(
        _kv_side_kernel,
        grid=(b, t // tm),
        in_specs=[
            pl.BlockSpec((1, tm, d), lambda bi, i: (bi, i, 0)),
            pl.BlockSpec((tm, LANES), lambda bi, i: (i, 0)),
            _resident((1, d)), _resident(wc.shape), _resident(wpe.shape), _resident((1, KV_LORA)),
            _resident(wuk.shape),
        ],
        out_specs=[
            pl.BlockSpec((1, tm, KV_LORA), lambda bi, i: (bi, i, 0)),
            pl.BlockSpec((1, QK_ROPE, tm), lambda bi, i: (bi, 0, i)),
            pl.BlockSpec((1, tm, KCAT), lambda bi, i: (bi, i, 0)),
            pl.BlockSpec((1, tm // kb, VROWS, kb), lambda bi, i: (bi, i, 0, 0)),
            pl.BlockSpec((1, N_HEADS, tm, KHEAD), lambda bi, i: (bi, 0, i, 0)),
        ],
        out_shape=[
            jax.ShapeDtypeStruct((b, t, KV_LORA), F32),
            jax.ShapeDtypeStruct((b, QK_ROPE, t), F32),
            jax.ShapeDtypeStruct((b, t, KCAT), BF16),
            jax.ShapeDtypeStruct((b, t // kb, VROWS, kb), BF16),
            jax.ShapeDtypeStruct((b, N_HEADS, t, KHEAD), BF16),
        ],
        compiler_params=_params(2),
        name="kv_side",
    )(x, cs_tab, g_in, wc, wpe, kvg, wuk)


def _q_rows(x, cos, sin, pre_g, wdq, qg, wuq_n, wuq_pa, wuq_pb, wuk_t, store):
    h = _rms(x, pre_g).astype(BF16)
    cq = _rms(_dot(h, wdq[...]), qg).astype(BF16)
    qn = _dot(cq, wuq_n[...])
    qa = _dot(cq, wuq_pa[...])
    qb = _dot(cq, wuq_pb[...])
    for hd in range(N_HEADS):
        sl = slice(hd * LANES, (hd + 1) * LANES)
        qpe = qa[:, sl] * cos + qb[:, sl] * sin
        ql = _dot(qn[:, sl].astype(BF16), wuk_t[hd])
        store(hd, (ql * ATTN_SCALE).astype(BF16), (qpe * ATTN_SCALE).astype(BF16))


def _mla_out(o_heads, x, post_g, wuv, wo, o_ref):
    for hd in range(N_HEADS):
        o_ref[:, hd * V_HEAD:(hd + 1) * V_HEAD] = _dot(o_heads(hd).astype(BF16), wuv[hd]).astype(BF16)
    m = _dot(o_ref[...], wo[...])
    return x + _rms(m, post_g)


def _mla_prompt_kernel(x_ref, k_ref, vt_ref, cos_ref, sin_ref, pre_g_ref, post_g_ref, wdq_ref, qg_ref,
                       wuqn_ref, wuqa_ref, wuqb_ref, wuvt_ref, wot_ref,
                       y_ref, qt_ref, m_ref, acc_ref, o_ref):
    tq = x_ref.shape[1]
    qi = pl.program_id(1)
    x = x_ref[0]

    h = _rms(x, pre_g_ref[...]).astype(BF16)
    cq = _rms(_dot(h, wdq_ref[...]), qg_ref[...])
    cq_t = cq.T.astype(BF16)
    qn_t = _dot(wuqn_ref[...], cq_t)
    qa_t = _dot(wuqa_ref[...], cq_t)
    qb_t = _dot(wuqb_ref[...], cq_t)
    cos_t, sin_t = cos_ref[...], sin_ref[...]
    for hd in range(N_HEADS):
        cols = slice(hd * tq, (hd + 1) * tq)
        rows = slice(hd * QK_ROPE, (hd + 1) * QK_ROPE)
        qpe_t = qa_t[rows] * cos_t + qb_t[rows] * sin_t
        qt_ref[:QK_NOPE, cols] = (qn_t[hd * QK_NOPE:(hd + 1) * QK_NOPE] * ATTN_SCALE).astype(BF16)
        qt_ref[QK_NOPE:QK_NOPE + QK_ROPE, cols] = (qpe_t * ATTN_SCALE).astype(BF16)
    qt_ref[QK_NOPE + QK_ROPE:, :] = jnp.zeros((KHEAD - QK_NOPE - QK_ROPE, N_HEADS * tq), BF16)

    def step(kb, diagonal, nblk=1):
        keys = pl.ds(pl.multiple_of(kb * tq, tq), nblk * tq)
        vt = jnp.concatenate([vt_ref[0, kb + j] for j in range(nblk)], axis=1)
        scores = [_dot(k_ref[0, hd, keys, :], qt_ref[:, hd * tq:(hd + 1) * tq])
                  for hd in range(N_HEADS)]
        for hd in range(N_HEADS):
            cols = slice(hd * tq, (hd + 1) * tq)
            st = scores[hd]
            if diagonal:
                key = lax.broadcasted_iota(jnp.int32, st.shape, 0)
                qry = lax.broadcasted_iota(jnp.int32, st.shape, 1)
                st = jnp.where(key <= qry, st, -jnp.inf)
                m_new = jnp.max(st, axis=0, keepdims=True)
                p = jnp.exp(st - m_new)
                acc_ref[:, cols] = _dot(vt, p.astype(BF16))
            else:
                m_old = m_ref[hd:hd + 1, :]
                m_new = jnp.maximum(m_old, jnp.max(st, axis=0, keepdims=True))
                alpha = jnp.exp(m_old - m_new)
                p = jnp.exp(st - m_new)
                acc_ref[:, cols] = acc_ref[:, cols] * alpha + _dot(vt, p.astype(BF16))
            m_ref[hd:hd + 1, :] = m_new

    def pair(j, carry):
        step(2 * j, False, nblk=2)
        return carry

    step(qi, True)
    lax.fori_loop(0, qi // 2, pair, 0)

    @pl.when(qi % 2 == 1)
    def _():
        step(qi - 1, False)

    for hd in range(N_HEADS):
        cols = slice(hd * tq, (hd + 1) * tq)
        o_t = acc_ref[:KV_LORA, cols] * (1.0 / acc_ref[KV_LORA:KV_LORA + 1, cols])
        o_ref[hd * V_HEAD:(hd + 1) * V_HEAD, :] = _dot(wuvt_ref[hd], o_t.astype(BF16)).astype(BF16)
    m_t = _dot(wot_ref[...], o_ref[...])
    y_ref[0] = x + _rms(m_t.T, post_g_ref[...])


def _mla_prompt(x, khead, ckv_t, cos_t, sin_t, pre_g, post_g, wdq, qg, wuqn_t, wuqa_t, wuqb_t,
                wuvt_h, wo_t):
    b, t, d = x.shape
    tq = TQ
    lanes = N_HEADS * tq
    weights = (wdq, qg, wuqn_t, wuqa_t, wuqb_t, wuvt_h, wo_t)
    return pl.pallas_call(
        _mla_prompt_kernel,
        grid=(b, t // tq),
        in_specs=[
            pl.BlockSpec((1, tq, d), lambda bi, i: (bi, i, 0)),
            pl.BlockSpec((1, N_HEADS, t, KHEAD), lambda bi, i: (bi, 0, 0, 0)),
            pl.BlockSpec((1, t // tq, VROWS, tq), lambda bi, i: (bi, 0, 0, 0)),
            pl.BlockSpec((QK_ROPE, tq), lambda bi, i: (0, i)),
            pl.BlockSpec((QK_ROPE, tq), lambda bi, i: (0, i)),
            _resident((1, d)), _resident((1, d)),
        ] + [_resident(w.shape) for w in weights],
        out_specs=pl.BlockSpec((1, tq, d), lambda bi, i: (bi, i, 0)),
        out_shape=jax.ShapeDtypeStruct((b, t, d), F32),
        scratch_shapes=[
            pltpu.VMEM((KHEAD, lanes), BF16),
            pltpu.VMEM((N_HEADS, tq), F32),
            pltpu.VMEM((VROWS, lanes), F32),
            pltpu.VMEM((N_HEADS * V_HEAD, tq), BF16),
        ],
        compiler_params=_params(2),
        name="mla_prompt",
    )(x, khead, ckv_t, cos_t, sin_t, pre_g, post_g, *weights)


def _q_sample_kernel(x_ref, cos_ref, sin_ref, pre_g_ref, wdq_ref, qg_ref, wuqn_ref, wuqa_ref, wuqb_ref,
                     wukt_ref, q_ref):
    def store(hd, ql, qpe):
        q_ref[hd, :, :KV_LORA] = ql
        q_ref[hd, :, KV_LORA:] = qpe

    _q_rows(x_ref[...], cos_ref[...], sin_ref[...], pre_g_ref[...], wdq_ref, qg_ref[...],
            wuqn_ref, wuqa_ref, wuqb_ref, wukt_ref, store)


def _q_sample(x, cos, sin, pre_g, wdq, qg, wuq_n, wuq_pa, wuq_pb, wuk_t):
    n = x.shape[0]
    return pl.pallas_call(
        _q_sample_kernel,
        out_shape=jax.ShapeDtypeStruct((N_HEADS, n, KCAT), BF16),
        compiler_params=pltpu.CompilerParams(vmem_limit_bytes=VMEM_LIMIT),
        name="q_sample",
    )(x, cos, sin, pre_g, wdq, qg, wuq_n, wuq_pa, wuq_pb, wuk_t)


def _decode_kernel(pt_ref, q_ref, knew_ref, ckv_hbm, krt_hbm, o_ref, cbuf, rbuf, sem):
    b = pl.program_id(0)
    nb = pl.num_programs(0)
    n_pages = cbuf.shape[1]
    n_chunks = n_pages // DEC_CHUNK_PAGES
    cp_rows = DEC_CHUNK_PAGES * PAGE_SIZE
    ahead = DEC_SLOTS - 1

    def page_copies(page, slot, p):
        return (pltpu.make_async_copy(ckv_hbm.at[page], cbuf.at[slot, p], sem.at[0, slot]),
                pltpu.make_async_copy(krt_hbm.at[page], rbuf.at[slot, p], sem.at[1, slot]))

    def start_pages(seq, slot, p_lo, p_hi):
        for p in range(p_lo, p_hi):
            for cp in page_copies(pt_ref[seq, p], slot, p):
                cp.start(priority=p % 2)

    def wait_slot(slot):
        for p in range(n_pages):
            for cp in page_copies(0, slot, p):
                cp.wait()

    slot = b % DEC_SLOTS
    nxt = jnp.minimum(b + ahead, nb - 1)
    nxt_slot = (b + ahead) % DEC_SLOTS

    @pl.when(b == 0)
    def _():
        for s in range(ahead):
            start_pages(s, s, 0, n_pages)

    wait_slot(slot)

    q = q_ref[0]
    ql = q[:, :KV_LORA]
    qp = q[:, KV_LORA:KV_LORA + QK_ROPE]

    def chunk_scores(c):
        start_pages(nxt, nxt_slot, c * DEC_CHUNK_PAGES, (c + 1) * DEC_CHUNK_PAGES)
        s_pe = jnp.concatenate(
            [_dot(qp, rbuf[slot, c * DEC_CHUNK_PAGES + j].astype(BF16)) for j in range(DEC_CHUNK_PAGES)],
            axis=1)
        rows = cbuf[slot, c * DEC_CHUNK_PAGES:(c + 1) * DEC_CHUNK_PAGES].reshape(cp_rows, KV_LORA)
        return _dot(ql, rows.T.astype(BF16)) + s_pe, rows.astype(BF16)

    kn = knew_ref[0].astype(F32)
    m = jnp.sum(q.astype(F32) * kn, axis=-1, keepdims=True)
    l = jnp.ones_like(m)
    acc = jnp.broadcast_to(kn[:, :KV_LORA], (N_HEADS, KV_LORA))
    nxt_chunk = chunk_scores(0)
    for c in range(n_chunks):
        s, kc = nxt_chunk
        if c + 1 < n_chunks:
            nxt_chunk = chunk_scores(c + 1)
        m_new = jnp.maximum(m, jnp.max(s, axis=-1, keepdims=True))
        alpha = jnp.exp(m - m_new)
        p = jnp.exp(s - m_new)
        l = l * alpha + jnp.sum(p, axis=-1, keepdims=True)
        acc = acc * alpha + _dot(p.astype(BF16), kc)
        m = m_new
    o_ref[0] = acc / l

    @pl.when(b == nb - 1)
    def _():
        for s in range(1, DEC_SLOTS):
            wait_slot((slot + s) % DEC_SLOTS)


def _decode(page_table, q, knew, cache_ckv, cache_krt):
    n, n_pages = page_table.shape
    grid_spec = pltpu.PrefetchScalarGridSpec(
        num_scalar_prefetch=1,
        grid=(n,),
        in_specs=[
            pl.BlockSpec((1, N_HEADS, KCAT), lambda bi, pt: (bi, 0, 0)),
            pl.BlockSpec((1, 1, KCAT), lambda bi, pt: (bi, 0, 0)),
            pl.BlockSpec(memory_space=pl.ANY),
            pl.BlockSpec(memory_space=pl.ANY),
        ],
        out_specs=pl.BlockSpec((1, N_HEADS, KV_LORA), lambda bi, pt: (bi, 0, 0)),
        scratch_shapes=[
            pltpu.VMEM((DEC_SLOTS, n_pages, PAGE_SIZE, KV_LORA), F32),
            pltpu.VMEM((DEC_SLOTS, n_pages, QK_ROPE, PAGE_SIZE), F32),
            pltpu.SemaphoreType.DMA((2, DEC_SLOTS)),
        ],
    )
    return pl.pallas_call(
        _decode_kernel,
        grid_spec=grid_spec,
        out_shape=jax.ShapeDtypeStruct((n, N_HEADS, KV_LORA), F32),
        compiler_params=_params(1),
        name="mla_decode",
    )(page_table, q, knew, cache_ckv, cache_krt)


def _out_sample_kernel(o_ref, x_ref, post_g_ref, wuv_ref, wo_ref, y_ref, os_ref):
    y_ref[...] = _mla_out(lambda hd: o_ref[hd], x_ref[...], post_g_ref[...], wuv_ref, wo_ref, os_ref)


def _out_sample(o_heads, x, post_g, wuv, wo):
    n, d = x.shape
    return pl.pallas_call(
        _out_sample_kernel,
        out_shape=jax.ShapeDtypeStruct((n, d), F32),
        scratch_shapes=[pltpu.VMEM((n, N_HEADS * V_HEAD), BF16)],
        compiler_params=pltpu.CompilerParams(vmem_limit_bytes=VMEM_LIMIT),
        name="out_sample",
    )(o_heads, x, post_g, wuv, wo)


def _rot_half_cols(w):
    half = QK_ROPE // 2
    return jnp.concatenate([-w[..., half:], w[..., :half]], axis=-1)


def _rope_tables(pos):
    half = QK_ROPE // 2
    inv = 1.0 / (ROPE_THETA ** (jnp.arange(half, dtype=F32) / half))
    ang = pos.astype(F32)[:, None] * inv[None, :]
    cos, sin = jnp.cos(ang), jnp.sin(ang)
    cos64 = jnp.concatenate([cos, cos], axis=-1)
    sin64 = jnp.concatenate([sin, sin], axis=-1)
    return cos64, sin64


def kernel(x_prompt, x_sample, cache_ckv, cache_kr, state_conv, page_table, pre_mix_g, post_mix_g,
           pre_ffn_g, post_ffn_g, w_in_a, sgu_g, w_s, b_s, w_out_a, kv_in_g, w_dkv, kv_g, w_uk, w_uv,
           w_dq, q_g, w_uq, w_o, w_up, conv_w, conv_b, w_down):
    depth = w_up.shape[0]
    n_a = w_in_a.shape[0]
    bp, tp, _ = x_prompt.shape
    ns = x_sample.shape[0]
    past_len = page_table.shape[1] * PAGE_SIZE
    row = lambda v: v.reshape(1, -1)

    w_in_b, w_out_b = w_in_a.astype(BF16), w_out_a.astype(BF16)
    w_up_b, w_down_b = w_up.astype(BF16), w_down.astype(BF16)
    w_dq_b, w_o_b = w_dq.astype(BF16), w_o.astype(BF16)
    wuq = w_uq.reshape(-1, Q_LORA, N_HEADS, QK_NOPE + QK_ROPE)
    wuq_n = wuq[..., :QK_NOPE].reshape(-1, Q_LORA, N_HEADS * QK_NOPE).astype(BF16)
    pad = lambda w: jnp.pad(w, ((0, 0),) * 3 + ((0, LANES - QK_ROPE),)).reshape(
        -1, Q_LORA, N_HEADS * LANES).astype(BF16)
    wuq_pa = pad(wuq[..., QK_NOPE:])
    wuq_pb = pad(_rot_half_cols(wuq[..., QK_NOPE:]))
    wuk_t = jnp.transpose(w_uk, (1, 2, 0)).astype(BF16)
    wuv_h = jnp.transpose(w_uv, (1, 0, 2)).astype(BF16)
    wuqn_t = jnp.swapaxes(wuq_n, 1, 2)
    flat_t = lambda w: jnp.swapaxes(w.reshape(-1, Q_LORA, N_HEADS * QK_ROPE), 1, 2).astype(BF16)
    wuqa_t = flat_t(wuq[..., QK_NOPE:])
    wuqb_t = flat_t(_rot_half_cols(wuq[..., QK_NOPE:]))
    wuk_all = w_uk.reshape(KV_LORA, N_HEADS * QK_NOPE).astype(BF16)
    wuvt_h = jnp.transpose(w_uv, (1, 2, 0)).astype(BF16)
    wo_t = jnp.swapaxes(w_o_b, 1, 2)
    cache_krt = jnp.swapaxes(cache_kr, 1, 2)
    w_c = w_dkv[:, :KV_LORA].astype(BF16)
    w_pe = w_dkv[:, KV_LORA:]
    w_pe2 = jnp.concatenate([w_pe, _rot_half_cols(w_pe)], axis=-1).astype(BF16)
    bs_col = b_s[..., None]
    wdiag = jnp.repeat(w_s[:, :, 0, 0], SGU_GROUP_DIM, axis=-1)
    bdiag = jnp.repeat(b_s[:, :, 0], SGU_GROUP_DIM, axis=-1)

    cos_p, sin_p = _rope_tables(jnp.arange(tp, dtype=jnp.int32))
    cos_s, sin_s = _rope_tables(jnp.full((ns,), past_len, dtype=jnp.int32))
    tile2 = lambda a: jnp.concatenate([a, a], axis=-1)
    cs_p = jnp.concatenate([cos_p, sin_p], axis=-1)
    cs_s = jnp.concatenate([cos_s, sin_s], axis=-1)

    xp = x_prompt
    xs = x_sample.reshape(ns, D_MODEL)
    conv_p, conv_s, v_rows = [], [], []
    khead_p = kcat_s = ckvt_p = ckv_p = kr_p = ckv_s = kr_s = None
    for layer in range(depth):
        if layer == n_a:
            kv_w = (row(kv_in_g), w_c, w_pe2, row(kv_g), wuk_all)
            ckv_p, kr_p, _, ckvt_p, khead_p = _kv_side(xp, cs_p, *kv_w, TM_KV)
            ckv_s, kr_s, kcat_s, _, _ = _kv_side(xs[None], cs_s, *kv_w, ns)
        pre_g, post_g = row(pre_mix_g[layer]), row(post_mix_g[layer])
        if layer < n_a:
            xp = _sgu_prompt(xp, pre_g, post_g, layer, w_in_b, row(sgu_g[layer]), w_s[layer],
                             bs_col[layer], w_out_b)
            xs, v = _sgu_sample(xs, pre_g, post_g, layer, w_in_b, row(sgu_g[layer]), row(wdiag[layer]),
                                row(bdiag[layer]), w_out_b)
            v_rows.append(v)
        else:
            j = layer - n_a
            qw = (w_dq_b[j], row(q_g[j]), wuq_n[j], wuq_pa[j], wuq_pb[j], wuk_t)
            xp = _mla_prompt(xp, khead_p, ckvt_p, cos_p.T, sin_p.T, pre_g, post_g, w_dq_b[j], row(q_g[j]),
                             wuqn_t[j], wuqa_t[j], wuqb_t[j], wuvt_h, wo_t[j])
            q = _q_sample(xs, tile2(cos_s), tile2(sin_s), pre_g, *qw)
            o = _decode(page_table, jnp.transpose(q, (1, 0, 2)), kcat_s.reshape(ns, 1, KCAT),
                        cache_ckv, cache_krt)
            xs = _out_sample(jnp.transpose(o, (1, 0, 2)), xs, post_g, wuv_h, w_o_b[j])
        pre_g, post_g = row(pre_ffn_g[layer]), row(post_ffn_g[layer])
        xp, tail = _ffn_prompt(xp, pre_g, post_g, layer, w_up_b, conv_w[layer], row(conv_b[layer]),
                               w_down_b)
        conv_p.append(tail)
        xs, st = _ffn_sample(xs, state_conv, pre_g, post_g, layer, w_up_b, conv_w[layer],
                             row(conv_b[layer]), w_down_b)
        conv_s.append(st)

    return (xp, xs.reshape(ns, 1, D_MODEL), ckv_p, jnp.swapaxes(kr_p, 1, 2), jnp.stack(conv_p),
            ckv_s.reshape(ns, 1, KV_LORA), jnp.swapaxes(kr_s, 1, 2).reshape(ns, 1, QK_ROPE), jnp.stack(conv_s),
            jnp.stack(v_rows).reshape(n_a, ns, 1, SGU_WIDTH))
```

```python
import jax
import jax.numpy as jnp
from jax import lax
from jax.experimental import pallas as pl
from jax.experimental.pallas import tpu as pltpu

F32 = jnp.float32
BF16 = jnp.bfloat16

D_MODEL = 1024
N_A_LAYERS = 2
CHUNK = 128
SGU_WIDTH = 2 * D_MODEL
SGU_GROUPS = 8
SGU_GROUP_DIM = SGU_WIDTH // SGU_GROUPS
N_HEADS = 8
QK_NOPE = 128
QK_ROPE = 64
V_HEAD = 128
Q_LORA = D_MODEL // 2
KV_LORA = D_MODEL // 4
ROPE_THETA = 10000.0
ATTN_SCALE = (QK_NOPE + QK_ROPE) ** -0.5
D_FF = 11 * D_MODEL // 4
CONV_W = 3
EPS = 1e-6
PAGE_SIZE = 128

LANES = 128
BF16_ROWS = 16
VMEM_LIMIT = 56 * 1024 * 1024

KCAT = KV_LORA + LANES
KHEAD = QK_NOPE + LANES
VROWS = KV_LORA + BF16_ROWS
FF_TILE = 256
N_FF = D_FF // FF_TILE
TM_FFN = 512
TM_SGU = 512
TQ = 256
TM_KV = 1024
DEC_CHUNK_PAGES = 8
DEC_SLOTS = 4


def _rms(x, g):
    return x * lax.rsqrt(jnp.mean(x * x, axis=-1, keepdims=True) + EPS) * g


def _gelu_erf(x):
    return 0.5 * x * (1.0 + lax.erf(x * (0.5 ** 0.5)))


def _dot(a, b):
    return jnp.dot(a, b, preferred_element_type=F32)


def _params(n_grid_axes):
    return pltpu.CompilerParams(
        dimension_semantics=("arbitrary",) * n_grid_axes, vmem_limit_bytes=VMEM_LIMIT)


def _resident(shape):
    nd = len(shape)
    return pl.BlockSpec(shape, lambda *_: (0,) * nd, pipeline_mode=pl.Buffered(1))


def _layer_resident(stacked, layer):
    nd = stacked.ndim - 1
    return pl.BlockSpec((None,) + stacked.shape[1:], lambda *_: (layer,) + (0,) * nd,
                        pipeline_mode=pl.Buffered(1))


def _ffn_prompt_kernel(x_ref, pre_g_ref, post_g_ref, wup_ref, cw_ref, cb_ref, wdn_ref,
                       y_ref, tail_ref, act_ref, h_ref):
    tm = x_ref.shape[1]
    ng = tm // 8
    i = pl.program_id(1)

    @pl.when(i == 0)
    def _():
        tail_ref[...] = jnp.zeros(tail_ref.shape, F32)

    @pl.when(i >= 0)
    def _():
        hn = _rms(x_ref[0], pre_g_ref[...])
        h_ref[...] = jnp.swapaxes(hn.reshape(8, ng, hn.shape[-1]), 0, 1).reshape(tm, -1).astype(BF16)

    first = lax.broadcasted_iota(jnp.int32, (8, FF_TILE), 0) == 0

    def conv(col0):
        sl = slice(col0, col0 + FF_TILE)
        ap = _dot(h_ref[...], wup_ref[:, sl])
        prev = tail_ref[0, :, sl]
        sh1 = jnp.where(first, pltpu.roll(prev[8:], 1, 0), pltpu.roll(ap[tm - 8:], 1, 0))
        sh2 = jnp.where(first, pltpu.roll(prev[:8], 1, 0), pltpu.roll(ap[tm - 16:tm - 8], 1, 0))
        w = cw_ref[:, sl]
        c = cb_ref[:, sl] + jnp.concatenate([sh2, sh1, ap[:tm - 16]], axis=0) * w[0:1]
        c = c + jnp.concatenate([sh1, ap[:tm - 8]], axis=0) * w[1:2]
        c = c + ap * w[2:3]
        tail_ref[0, :, sl] = ap[tm - 16:]
        return c

    for f in range(N_FF):
        cg = conv(f * FF_TILE)
        cu = conv(D_FF + f * FF_TILE)
        act_ref[:, f * FF_TILE:(f + 1) * FF_TILE] = (jax.nn.gelu(cg, approximate=True) * cu).astype(BF16)
    fo = _rms(_dot(act_ref[...], wdn_ref[...]), post_g_ref[...])
    y_ref[0] = x_ref[0] + jnp.swapaxes(fo.reshape(ng, 8, fo.shape[-1]), 0, 1).reshape(tm, -1)


def _ffn_prompt(x, pre_g, post_g, layer, wup, cw, cb, wdn):
    b, t, d = x.shape
    tm = TM_FFN
    y, tail = pl.pallas_call(
        _ffn_prompt_kernel,
        grid=(b, t // tm),
        in_specs=[
            pl.BlockSpec((1, tm, d), lambda bi, i: (bi, i, 0)),
            _resident((1, d)), _resident((1, d)),
            _layer_resident(wup, layer), _resident(cw.shape), _resident(cb.shape),
            _layer_resident(wdn, layer),
        ],
        out_specs=[
            pl.BlockSpec((1, tm, d), lambda bi, i: (bi, i, 0)),
            pl.BlockSpec((1, 16, 2 * D_FF), lambda bi, i: (bi, 0, 0)),
        ],
        out_shape=[
            jax.ShapeDtypeStruct((b, t, d), F32),
            jax.ShapeDtypeStruct((b, 16, 2 * D_FF), F32),
        ],
        scratch_shapes=[pltpu.VMEM((tm, D_FF), BF16), pltpu.VMEM((tm, d), BF16)],
        compiler_params=_params(2),
        name="ffn_prompt",
    )(x, pre_g, post_g, wup, cw, cb, wdn)
    return y, tail[:, 7::8, :]


def _ffn_sample_kernel(x_ref, sg_ref, su_ref, pre_g_ref, post_g_ref,
                       wg_ref, wu_ref, cwg_ref, cwu_ref, cbg_ref, cbu_ref, wdn_ref,
                       y_ref, og_ref, ou_ref, h_ref, acc_ref):
    f = pl.program_id(0)

    @pl.when(f == 0)
    def _():
        h_ref[...] = _rms(x_ref[...], pre_g_ref[...]).astype(BF16)

    h = h_ref[...]

    def conv(w_ref, s_ref, cw_ref, cb_ref, o_ref):
        a = _dot(h, w_ref[...])
        p0, p1 = s_ref[:, 0, :], s_ref[:, 1, :]
        o_ref[:, 0, :] = p1
        o_ref[:, 1, :] = a
        w = cw_ref[...]
        c = cb_ref[...] + p0 * w[0:1]
        c = c + p1 * w[1:2]
        return c + a * w[2:3]

    cg = conv(wg_ref, sg_ref, cwg_ref, cbg_ref, og_ref)
    cu = conv(wu_ref, su_ref, cwu_ref, cbu_ref, ou_ref)
    act = (jax.nn.gelu(cg, approximate=True) * cu).astype(BF16)
    part = _dot(act, wdn_ref[...])

    @pl.when(f == 0)
    def _():
        acc_ref[...] = part

    @pl.when(f > 0)
    def _():
        acc_ref[...] += part

    @pl.when(f == pl.num_programs(0) - 1)
    def _():
        y_ref[...] = x_ref[...] + _rms(acc_ref[...], post_g_ref[...])


def _ffn_sample(x, state_all, pre_g, post_g, layer, wup, cw, cb, wdn):
    n, d = x.shape
    tf = FF_TILE
    hrows = CONV_W - 1
    state = lambda off: pl.BlockSpec((None, n, hrows, tf), lambda f: (layer, 0, 0, off + f))
    row3 = lambda off: pl.BlockSpec((CONV_W, tf), lambda f: (0, off + f))
    row1 = lambda off: pl.BlockSpec((1, tf), lambda f: (0, off + f))
    const = lambda shape: pl.BlockSpec(shape, lambda f: (0,) * len(shape))
    half = pl.BlockSpec((n, hrows, tf), lambda f: (0, 0, f))
    y, og, ou = pl.pallas_call(
        _ffn_sample_kernel,
        grid=(N_FF,),
        in_specs=[
            const((n, d)),
            state(0), state(N_FF),
            const((1, d)), const((1, d)),
            pl.BlockSpec((None, d, tf), lambda f: (layer, 0, f)),
            pl.BlockSpec((None, d, tf), lambda f: (layer, 0, N_FF + f)),
            row3(0), row3(N_FF), row1(0), row1(N_FF),
            pl.BlockSpec((None, tf, d), lambda f: (layer, f, 0)),
        ],
        out_specs=[const((n, d)), half, half],
        out_shape=[
            jax.ShapeDtypeStruct((n, d), F32),
            jax.ShapeDtypeStruct((n, hrows, D_FF), F32),
            jax.ShapeDtypeStruct((n, hrows, D_FF), F32),
        ],
        scratch_shapes=[pltpu.VMEM((n, d), BF16), pltpu.VMEM((n, d), F32)],
        compiler_params=_params(1),
        name="ffn_sample",
    )(x, state_all, state_all, pre_g, post_g, wup, wup, cw, cw, cb, cb, wdn)
    return y, jnp.concatenate([og, ou], axis=-1)


def _sgu_prompt_kernel(x_ref, pre_g_ref, post_g_ref, win_ref, sgug_ref, ws_ref, bs_ref, wout_ref,
                       y_ref, v_ref, gated_ref, h_ref):
    tm = x_ref.shape[1]
    gd = SGU_GROUP_DIM
    x = x_ref[0]

    @pl.when(pl.program_id(1) >= 0)
    def _():
        h_ref[...] = _rms(x, pre_g_ref[...]).astype(BF16)

    ss = jnp.zeros((tm, 1), F32)
    for j in range(SGU_GROUPS):
        zc = _gelu_erf(_dot(h_ref[...], win_ref[:, SGU_WIDTH + j * gd:SGU_WIDTH + (j + 1) * gd]))
        v_ref[:, j * gd:(j + 1) * gd] = zc
        ss = ss + jnp.sum(zc * zc, axis=-1, keepdims=True)
    inv = lax.rsqrt(ss * (1.0 / SGU_WIDTH) + EPS)
    causal = (lax.broadcasted_iota(jnp.int32, (CHUNK, CHUNK), 0)
              >= lax.broadcasted_iota(jnp.int32, (CHUNK, CHUNK), 1))
    for g in range(SGU_GROUPS):
        sl = slice(g * gd, (g + 1) * gd)
        vn = (v_ref[:, sl] * inv * sgug_ref[:, sl]).astype(BF16)
        wm = jnp.where(causal, ws_ref[g], 0.0).astype(BF16)
        bias = bs_ref[g]
        s = jnp.concatenate(
            [_dot(wm, vn[c * CHUNK:(c + 1) * CHUNK]) + bias for c in range(tm // CHUNK)], axis=0)
        u = _gelu_erf(_dot(h_ref[...], win_ref[:, sl]))
        gated_ref[:, sl] = (u * s).astype(BF16)
    y_ref[0] = x + _rms(_dot(gated_ref[...], wout_ref[...]), post_g_ref[...])


def _sgu_prompt(x, pre_g, post_g, layer, win, sgug, ws, bs, wout):
    b, t, d = x.shape
    tm = TM_SGU
    return pl.pallas_call(
        _sgu_prompt_kernel,
        grid=(b, t // tm),
        in_specs=[
            pl.BlockSpec((1, tm, d), lambda bi, i: (bi, i, 0)),
            _resident((1, d)), _resident((1, d)),
            _layer_resident(win, layer), _resident(sgug.shape), _resident(ws.shape), _resident(bs.shape),
            _layer_resident(wout, layer),
        ],
        out_specs=pl.BlockSpec((1, tm, d), lambda bi, i: (bi, i, 0)),
        out_shape=jax.ShapeDtypeStruct((b, t, d), F32),
        scratch_shapes=[pltpu.VMEM((tm, SGU_WIDTH), F32), pltpu.VMEM((tm, SGU_WIDTH), BF16),
                        pltpu.VMEM((tm, d), BF16)],
        compiler_params=_params(2),
        name="sgu_prompt",
    )(x, pre_g, post_g, win, sgug, ws, bs, wout)


def _sgu_sample_kernel(x_ref, pre_g_ref, post_g_ref, win_ref, sgug_ref, wdiag_ref, bdiag_ref, wout_ref,
                       y_ref, v_ref):
    x = x_ref[...]
    h = _rms(x, pre_g_ref[...]).astype(BF16)
    u = _gelu_erf(_dot(h, win_ref[:, :SGU_WIDTH]))
    v = _gelu_erf(_dot(h, win_ref[:, SGU_WIDTH:]))
    vn = _rms(v, sgug_ref[...])
    v_ref[...] = vn
    s = vn * wdiag_ref[...] + bdiag_ref[...]
    y = _dot((u * s).astype(BF16), wout_ref[...])
    y_ref[...] = x + _rms(y, post_g_ref[...])


def _sgu_sample(x, pre_g, post_g, layer, win, sgug, wdiag, bdiag, wout):
    n, d = x.shape
    whole = lambda shape: pl.BlockSpec(shape, lambda i: (0,) * len(shape))
    return pl.pallas_call(
        _sgu_sample_kernel,
        grid=(1,),
        in_specs=[whole((n, d)), whole((1, d)), whole((1, d)), _layer_resident(win, layer),
                  whole((1, SGU_WIDTH)), whole((1, SGU_WIDTH)), whole((1, SGU_WIDTH)),
                  _layer_resident(wout, layer)],
        out_specs=[whole((n, d)), whole((n, SGU_WIDTH))],
        out_shape=[jax.ShapeDtypeStruct((n, d), F32), jax.ShapeDtypeStruct((n, SGU_WIDTH), F32)],
        compiler_params=_params(1),
        name="sgu_sample",
    )(x, pre_g, post_g, win, sgug, wdiag, bdiag, wout)


def _kv_side_kernel(x_ref, cs_ref, g_in_ref, wc_ref, wpe_ref, kvg_ref, wuk_ref,
                    ckv_ref, kr_ref, kcat_ref, ckvt_ref, khead_ref):
    h = _rms(x_ref[0], g_in_ref[...]).astype(BF16)
    ckv = _rms(_dot(h, wc_ref[...]), kvg_ref[...])
    t = _dot(h, wpe_ref[...]) * cs_ref[...]
    t = t + pltpu.roll(t, QK_ROPE, 1)
    lane = lax.broadcasted_iota(jnp.int32, t.shape, 1)
    krp = jnp.where(lane < QK_ROPE, t, 0.0)
    ckv_ref[0] = ckv
    kr_ref[0] = t.T[:QK_ROPE]
    kcat_ref[0] = jnp.concatenate([ckv, krp], axis=1).astype(BF16)
    kb = ckvt_ref.shape[-1]
    ones_row = (lax.broadcasted_iota(jnp.int32, (VROWS - KV_LORA, kb), 0) == 0).astype(F32)
    for j in range(ckvt_ref.shape[1]):
        ckvt_ref[0, j] = jnp.concatenate([ckv[j * kb:(j + 1) * kb].T, ones_row], axis=0).astype(BF16)
    kn = _dot(ckv.astype(BF16), wuk_ref[...])
    for hd in range(N_HEADS):
        khead_ref[0, hd] = jnp.concatenate(
            [kn[:, hd * QK_NOPE:(hd + 1) * QK_NOPE], krp], axis=1).astype(BF16)


def _kv_side(x, cs_tab, g_in, wc, wpe, kvg, wuk, tm):
    b, t, d = x.shape
    kb = min(tm, TQ)
    return pl.pallas_call(
        _kv_side_kernel,
        grid=(b, t // tm),
        in_specs=[
            pl.BlockSpec((1, tm, d), lambda bi, i: (bi, i, 0)),
            pl.BlockSpec((tm, LANES), lambda bi, i: (i, 0)),
            _resident((1, d)), _resident(wc.shape), _resident(wpe.shape), _resident((1, KV_LORA)),
            _resident(wuk.shape),
        ],
        out_specs=[
            pl.BlockSpec((1, tm, KV_LORA), lambda bi, i: (bi, i, 0)),
            pl.BlockSpec((1, QK_ROPE, tm), lambda bi, i: (bi, 0, i)),
            pl.BlockSpec((1, tm, KCAT), lambda bi, i: (bi, i, 0)),
            pl.BlockSpec((1, tm // kb, VROWS, kb), lambda bi, i: (bi, i, 0, 0)),
            pl.BlockSpec((1, N_HEADS, tm, KHEAD), lambda bi, i: (bi, 0, i, 0)),
        ],
        out_shape=[
            jax.ShapeDtypeStruct((b, t, KV_LORA), F32),
            jax.ShapeDtypeStruct((b, QK_ROPE, t), F32),
            jax.ShapeDtypeStruct((b, t, KCAT), BF16),
            jax.ShapeDtypeStruct((b, t // kb, VROWS, kb), BF16),
            jax.ShapeDtypeStruct((b, N_HEADS, t, KHEAD), BF16),
        ],
        compiler_params=_params(2),
        name="kv_side",
    )(x, cs_tab, g_in, wc, wpe, kvg, wuk)


def _q_rows(x, cos, sin, pre_g, wdq, qg, wuq_n, wuq_pa, wuq_pb, wuk_t, store):
    h = _rms(x, pre_g).astype(BF16)
    cq = _rms(_dot(h, wdq[...]), qg).astype(BF16)
    qn = _dot(cq, wuq_n[...])
    qa = _dot(cq, wuq_pa[...])
    qb = _dot(cq, wuq_pb[...])
    for hd in range(N_HEADS):
        sl = slice(hd * LANES, (hd + 1) * LANES)
        qpe = qa[:, sl] * cos + qb[:, sl] * sin
        ql = _dot(qn[:, sl].astype(BF16), wuk_t[hd])
        store(hd, (ql * ATTN_SCALE).astype(BF16), (qpe * ATTN_SCALE).astype(BF16))


def _mla_out(o_heads, x, post_g, wuv, wo, o_ref):
    for hd in range(N_HEADS):
        o_ref[:, hd * V_HEAD:(hd + 1) * V_HEAD] = _dot(o_heads(hd).astype(BF16), wuv[hd]).astype(BF16)
    m = _dot(o_ref[...], wo[...])
    return x + _rms(m, post_g)


def _mla_prompt_kernel(x_ref, k_ref, vt_ref, cos_ref, sin_ref, pre_g_ref, post_g_ref, wdq_ref, qg_ref,
                       wuqn_ref, wuqa_ref, wuqb_ref, wuvt_ref, wot_ref,
                       y_ref, qt_ref, m_ref, acc_ref, o_ref):
    tq = x_ref.shape[1]
    qi = pl.program_id(1)
    x = x_ref[0]

    h = _rms(x, pre_g_ref[...]).astype(BF16)
    cq = _rms(_dot(h, wdq_ref[...]), qg_ref[...])
    cq_t = cq.T.astype(BF16)
    qn_t = _dot(wuqn_ref[...], cq_t)
    qa_t = _dot(wuqa_ref[...], cq_t)
    qb_t = _dot(wuqb_ref[...], cq_t)
    cos_t, sin_t = cos_ref[...], sin_ref[...]
    for hd in range(N_HEADS):
        cols = slice(hd * tq, (hd + 1) * tq)
        rows = slice(hd * QK_ROPE, (hd + 1) * QK_ROPE)
        qpe_t = qa_t[rows] * cos_t + qb_t[rows] * sin_t
        qt_ref[:QK_NOPE, cols] = (qn_t[hd * QK_NOPE:(hd + 1) * QK_NOPE] * ATTN_SCALE).astype(BF16)
        qt_ref[QK_NOPE:QK_NOPE + QK_ROPE, cols] = (qpe_t * ATTN_SCALE).astype(BF16)
    qt_ref[QK_NOPE + QK_ROPE:, :] = jnp.zeros((KHEAD - QK_NOPE - QK_ROPE, N_HEADS * tq), BF16)

    def step(kb, diagonal, nblk=1):
        keys = pl.ds(pl.multiple_of(kb * tq, tq), nblk * tq)
        vt = jnp.concatenate([vt_ref[0, kb + j] for j in range(nblk)], axis=1)
        scores = [_dot(k_ref[0, hd, keys, :], qt_ref[:, hd * tq:(hd + 1) * tq])
                  for hd in range(N_HEADS)]
        for hd in range(N_HEADS):
            cols = slice(hd * tq, (hd + 1) * tq)
            st = scores[hd]
            if diagonal:
                key = lax.broadcasted_iota(jnp.int32, st.shape, 0)
                qry = lax.broadcasted_iota(jnp.int32, st.shape, 1)
                st = jnp.where(key <= qry, st, -jnp.inf)
                m_new = jnp.max(st, axis=0, keepdims=True)
                p = jnp.exp(st - m_new)
                acc_ref[:, cols] = _dot(vt, p.astype(BF16))
            else:
                m_old = m_ref[hd:hd + 1, :]
                m_new = jnp.maximum(m_old, jnp.max(st, axis=0, keepdims=True))
                alpha = jnp.exp(m_old - m_new)
                p = jnp.exp(st - m_new)
                acc_ref[:, cols] = acc_ref[:, cols] * alpha + _dot(vt, p.astype(BF16))
            m_ref[hd:hd + 1, :] = m_new

    def pair(j, carry):
        step(2 * j, False, nblk=2)
        return carry

    step(qi, True)
    lax.fori_loop(0, qi // 2, pair, 0)

    @pl.when(qi % 2 == 1)
    def _():
        step(qi - 1, False)

    for hd in range(N_HEADS):
        cols = slice(hd * tq, (hd + 1) * tq)
        o_t = acc_ref[:KV_LORA, cols] * (1.0 / acc_ref[KV_LORA:KV_LORA + 1, cols])
        o_ref[hd * V_HEAD:(hd + 1) * V_HEAD, :] = _dot(wuvt_ref[hd], o_t.astype(BF16)).astype(BF16)
    m_t = _dot(wot_ref[...], o_ref[...])
    y_ref[0] = x + _rms(m_t.T, post_g_ref[...])


def _mla_prompt(x, khead, ckv_t, cos_t, sin_t, pre_g, post_g, wdq, qg, wuqn_t, wuqa_t, wuqb_t,
                wuvt_h, wo_t):
    b, t, d = x.shape
    tq = TQ
    lanes = N_HEADS * tq
    weights = (wdq, qg, wuqn_t, wuqa_t, wuqb_t, wuvt_h, wo_t)
    return pl.pallas_call(
        _mla_prompt_kernel,
        grid=(b, t // tq),
        in_specs=[
            pl.BlockSpec((1, tq, d), lambda bi, i: (bi, i, 0)),
            pl.BlockSpec((1, N_HEADS, t, KHEAD), lambda bi, i: (bi, 0, 0, 0)),
            pl.BlockSpec((1, t // tq, VROWS, tq), lambda bi, i: (bi, 0, 0, 0)),
            pl.BlockSpec((QK_ROPE, tq), lambda bi, i: (0, i)),
            pl.BlockSpec((QK_ROPE, tq), lambda bi, i: (0, i)),
            _resident((1, d)), _resident((1, d)),
        ] + [_resident(w.shape) for w in weights],
        out_specs=pl.BlockSpec((1, tq, d), lambda bi, i: (bi, i, 0)),
        out_shape=jax.ShapeDtypeStruct((b, t, d), F32),
        scratch_shapes=[
            pltpu.VMEM((KHEAD, lanes), BF16),
            pltpu.VMEM((N_HEADS, tq), F32),
            pltpu.VMEM((VROWS, lanes), F32),
            pltpu.VMEM((N_HEADS * V_HEAD, tq), BF16),
        ],
        compiler_params=_params(2),
        name="mla_prompt",
    )(x, khead, ckv_t, cos_t, sin_t, pre_g, post_g, *weights)


def _q_sample_kernel(x_ref, cos_ref, sin_ref, pre_g_ref, wdq_ref, qg_ref, wuqn_ref, wuqa_ref, wuqb_ref,
                     wukt_ref, q_ref):
    def store(hd, ql, qpe):
        q_ref[hd, :, :KV_LORA] = ql
        q_ref[hd, :, KV_LORA:] = qpe

    _q_rows(x_ref[...], cos_ref[...], sin_ref[...], pre_g_ref[...], wdq_ref, qg_ref[...],
            wuqn_ref, wuqa_ref, wuqb_ref, wukt_ref, store)


def _q_sample(x, cos, sin, pre_g, wdq, qg, wuq_n, wuq_pa, wuq_pb, wuk_t):
    n = x.shape[0]
    return pl.pallas_call(
        _q_sample_kernel,
        out_shape=jax.ShapeDtypeStruct((N_HEADS, n, KCAT), BF16),
        compiler_params=pltpu.CompilerParams(vmem_limit_bytes=VMEM_LIMIT),
        name="q_sample",
    )(x, cos, sin, pre_g, wdq, qg, wuq_n, wuq_pa, wuq_pb, wuk_t)


def _decode_kernel(pt_ref, q_ref, knew_ref, ckv_hbm, krt_hbm, o_ref, cbuf, rbuf, sem):
    b = pl.program_id(0)
    nb = pl.num_programs(0)
    n_pages = cbuf.shape[1]
    n_chunks = n_pages // DEC_CHUNK_PAGES
    cp_rows = DEC_CHUNK_PAGES * PAGE_SIZE
    ahead = DEC_SLOTS - 1

    def page_copies(page, slot, p):
        return (pltpu.make_async_copy(ckv_hbm.at[page], cbuf.at[slot, p], sem.at[0, slot]),
                pltpu.make_async_copy(krt_hbm.at[page], rbuf.at[slot, p], sem.at[1, slot]))

    def start_pages(seq, slot, p_lo, p_hi):
        for p in range(p_lo, p_hi):
            for cp in page_copies(pt_ref[seq, p], slot, p):
                cp.start(priority=p % 2)

    def wait_slot(slot):
        for p in range(n_pages):
            for cp in page_copies(0, slot, p):
                cp.wait()

    slot = b % DEC_SLOTS
    nxt = jnp.minimum(b + ahead, nb - 1)
    nxt_slot = (b + ahead) % DEC_SLOTS

    @pl.when(b == 0)
    def _():
        for s in range(ahead):
            start_pages(s, s, 0, n_pages)

    wait_slot(slot)

    q = q_ref[0]
    ql = q[:, :KV_LORA]
    qp = q[:, KV_LORA:KV_LORA + QK_ROPE]

    def chunk_scores(c):
        start_pages(nxt, nxt_slot, c * DEC_CHUNK_PAGES, (c + 1) * DEC_CHUNK_PAGES)
        s_pe = jnp.concatenate(
            [_dot(qp, rbuf[slot, c * DEC_CHUNK_PAGES + j].astype(BF16)) for j in range(DEC_CHUNK_PAGES)],
            axis=1)
        rows = cbuf[slot, c * DEC_CHUNK_PAGES:(c + 1) * DEC_CHUNK_PAGES].reshape(cp_rows, KV_LORA)
        return _dot(ql, rows.T.astype(BF16)) + s_pe, rows.astype(BF16)

    kn = knew_ref[0].astype(F32)
    m = jnp.sum(q.astype(F32) * kn, axis=-1, keepdims=True)
    l = jnp.ones_like(m)
    acc = jnp.broadcast_to(kn[:, :KV_LORA], (N_HEADS, KV_LORA))
    nxt_chunk = chunk_scores(0)
    for c in range(n_chunks):
        s, kc = nxt_chunk
        if c + 1 < n_chunks:
            nxt_chunk = chunk_scores(c + 1)
        m_new = jnp.maximum(m, jnp.max(s, axis=-1, keepdims=True))
        alpha = jnp.exp(m - m_new)
        p = jnp.exp(s - m_new)
        l = l * alpha + jnp.sum(p, axis=-1, keepdims=True)
        acc = acc * alpha + _dot(p.astype(BF16), kc)
        m = m_new
    o_ref[0] = acc / l

    @pl.when(b == nb - 1)
    def _():
        for s in range(1, DEC_SLOTS):
            wait_slot((slot + s) % DEC_SLOTS)


def _decode(page_table, q, knew, cache_ckv, cache_krt):
    n, n_pages = page_table.shape
    grid_spec = pltpu.PrefetchScalarGridSpec(
        num_scalar_prefetch=1,
        grid=(n,),
        in_specs=[
            pl.BlockSpec((1, N_HEADS, KCAT), lambda bi, pt: (bi, 0, 0)),
            pl.BlockSpec((1, 1, KCAT), lambda bi, pt: (bi, 0, 0)),
            pl.BlockSpec(memory_space=pl.ANY),
            pl.BlockSpec(memory_space=pl.ANY),
        ],
        out_specs=pl.BlockSpec((1, N_HEADS, KV_LORA), lambda bi, pt: (bi, 0, 0)),
        scratch_shapes=[
            pltpu.VMEM((DEC_SLOTS, n_pages, PAGE_SIZE, KV_LORA), F32),
            pltpu.VMEM((DEC_SLOTS, n_pages, QK_ROPE, PAGE_SIZE), F32),
            pltpu.SemaphoreType.DMA((2, DEC_SLOTS)),
        ],
    )
    return pl.pallas_call(
        _decode_kernel,
        grid_spec=grid_spec,
        out_shape=jax.ShapeDtypeStruct((n, N_HEADS, KV_LORA), F32),
        compiler_params=_params(1),
        name="mla_decode",
    )(page_table, q, knew, cache_ckv, cache_krt)


def _out_sample_kernel(o_ref, x_ref, post_g_ref, wuv_ref, wo_ref, y_ref, os_ref):
    y_ref[...] = _mla_out(lambda hd: o_ref[hd], x_ref[...], post_g_ref[...], wuv_ref, wo_ref, os_ref)


def _out_sample(o_heads, x, post_g, wuv, wo):
    n, d = x.shape
    return pl.pallas_call(
        _out_sample_kernel,
        out_shape=jax.ShapeDtypeStruct((n, d), F32),
        scratch_shapes=[pltpu.VMEM((n, N_HEADS * V_HEAD), BF16)],
        compiler_params=pltpu.CompilerParams(vmem_limit_bytes=VMEM_LIMIT),
        name="out_sample",
    )(o_heads, x, post_g, wuv, wo)


def _rot_half_cols(w):
    half = QK_ROPE // 2
    return jnp.concatenate([-w[..., half:], w[..., :half]], axis=-1)


def _rope_tables(pos):
    half = QK_ROPE // 2
    inv = 1.0 / (ROPE_THETA ** (jnp.arange(half, dtype=F32) / half))
    ang = pos.astype(F32)[:, None] * inv[None, :]
    cos, sin = jnp.cos(ang), jnp.sin(ang)
    cos64 = jnp.concatenate([cos, cos], axis=-1)
    sin64 = jnp.concatenate([sin, sin], axis=-1)
    return cos64, sin64


def kernel(x_prompt, x_sample, cache_ckv, cache_kr, state_conv, page_table, pre_mix_g, post_mix_g,
           pre_ffn_g, post_ffn_g, w_in_a, sgu_g, w_s, b_s, w_out_a, kv_in_g, w_dkv, kv_g, w_uk, w_uv,
           w_dq, q_g, w_uq, w_o, w_up, conv_w, conv_b, w_down):
    depth = w_up.shape[0]
    n_a = w_in_a.shape[0]
    bp, tp, _ = x_prompt.shape
    ns = x_sample.shape[0]
    past_len = page_table.shape[1] * PAGE_SIZE
    row = lambda v: v.reshape(1, -1)

    w_in_b, w_out_b = w_in_a.astype(BF16), w_out_a.astype(BF16)
    w_up_b, w_down_b = w_up.astype(BF16), w_down.astype(BF16)
    w_dq_b, w_o_b = w_dq.astype(BF16), w_o.astype(BF16)
    wuq = w_uq.reshape(-1, Q_LORA, N_HEADS, QK_NOPE + QK_ROPE)
    wuq_n = wuq[..., :QK_NOPE].reshape(-1, Q_LORA, N_HEADS * QK_NOPE).astype(BF16)
    pad = lambda w: jnp.pad(w, ((0, 0),) * 3 + ((0, LANES - QK_ROPE),)).reshape(
        -1, Q_LORA, N_HEADS * LANES).astype(BF16)
    wuq_pa = pad(wuq[..., QK_NOPE:])
    wuq_pb = pad(_rot_half_cols(wuq[..., QK_NOPE:]))
    wuk_t = jnp.transpose(w_uk, (1, 2, 0)).astype(BF16)
    wuv_h = jnp.transpose(w_uv, (1, 0, 2)).astype(BF16)
    wuqn_t = jnp.swapaxes(wuq_n, 1, 2)
    flat_t = lambda w: jnp.swapaxes(w.reshape(-1, Q_LORA, N_HEADS * QK_ROPE), 1, 2).astype(BF16)
    wuqa_t = flat_t(wuq[..., QK_NOPE:])
    wuqb_t = flat_t(_rot_half_cols(wuq[..., QK_NOPE:]))
    wuk_all = w_uk.reshape(KV_LORA, N_HEADS * QK_NOPE).astype(BF16)
    wuvt_h = jnp.transpose(w_uv, (1, 2, 0)).astype(BF16)
    wo_t = jnp.swapaxes(w_o_b, 1, 2)
    cache_krt = jnp.swapaxes(cache_kr, 1, 2)
    w_c = w_dkv[:, :KV_LORA].astype(BF16)
    w_pe = w_dkv[:, KV_LORA:]
    w_pe2 = jnp.concatenate([w_pe, _rot_half_cols(w_pe)], axis=-1).astype(BF16)
    bs_col = b_s[..., None]
    wdiag = jnp.repeat(w_s[:, :, 0, 0], SGU_GROUP_DIM, axis=-1)
    bdiag = jnp.repeat(b_s[:, :, 0], SGU_GROUP_DIM, axis=-1)

    cos_p, sin_p = _rope_tables(jnp.arange(tp, dtype=jnp.int32))
    cos_s, sin_s = _rope_tables(jnp.full((ns,), past_len, dtype=jnp.int32))
    tile2 = lambda a: jnp.concatenate([a, a], axis=-1)
    cs_p = jnp.concatenate([cos_p, sin_p], axis=-1)
    cs_s = jnp.concatenate([cos_s, sin_s], axis=-1)

    xp = x_prompt
    xs = x_sample.reshape(ns, D_MODEL)
    conv_p, conv_s, v_rows = [], [], []
    khead_p = kcat_s = ckvt_p = ckv_p = kr_p = ckv_s = kr_s = None
    for layer in range(depth):
        if layer == n_a:
            kv_w = (row(kv_in_g), w_c, w_pe2, row(kv_g), wuk_all)
            ckv_p, kr_p, _, ckvt_p, khead_p = _kv_side(xp, cs_p, *kv_w, TM_KV)
            ckv_s, kr_s, kcat_s, _, _ = _kv_side(xs[None], cs_s, *kv_w, ns)
        pre_g, post_g = row(pre_mix_g[layer]), row(post_mix_g[layer])
        if layer < n_a:
            xp = _sgu_prompt(xp, pre_g, post_g, layer, w_in_b, row(sgu_g[layer]), w_s[layer],
                             bs_col[layer], w_out_b)
            xs, v = _sgu_sample(xs, pre_g, post_g, layer, w_in_b, row(sgu_g[layer]), row(wdiag[layer]),
                                row(bdiag[layer]), w_out_b)
            v_rows.append(v)
        else:
            j = layer - n_a
            qw = (w_dq_b[j], row(q_g[j]), wuq_n[j], wuq_pa[j], wuq_pb[j], wuk_t)
            xp = _mla_prompt(xp, khead_p, ckvt_p, cos_p.T, sin_p.T, pre_g, post_g, w_dq_b[j], row(q_g[j]),
                             wuqn_t[j], wuqa_t[j], wuqb_t[j], wuvt_h, wo_t[j])
            q = _q_sample(xs, tile2(cos_s), tile2(sin_s), pre_g, *qw)
            o = _decode(page_table, jnp.transpose(q, (1, 0, 2)), kcat_s.reshape(ns, 1, KCAT),
                        cache_ckv, cache_krt)
            xs = _out_sample(jnp.transpose(o, (1, 0, 2)), xs, post_g, wuv_h, w_o_b[j])
        pre_g, post_g = row(pre_ffn_g[layer]), row(post_ffn_g[layer])
        xp, tail = _ffn_prompt(xp, pre_g, post_g, layer, w_up_b, conv_w[layer], row(conv_b[layer]),
                               w_down_b)
        conv_p.append(tail)
        xs, st = _ffn_sample(xs, state_conv, pre_g, post_g, layer, w_up_b, conv_w[layer],
                             row(conv_b[layer]), w_down_b)
        conv_s.append(st)

    return (xp, xs.reshape(ns, 1, D_MODEL), ckv_p, jnp.swapaxes(kr_p, 1, 2), jnp.stack(conv_p),
            ckv_s.reshape(ns, 1, KV_LORA), jnp.swapaxes(kr_s, 1, 2).reshape(ns, 1, QK_ROPE), jnp.stack(conv_s),
            jnp.stack(v_rows).reshape(n_a, ns, 1, SGU_WIDTH))
```
